```python
import jax, jax.numpy as jnp
from jax import lax
import numpy as np

D_MODEL = 2048
BATCH = 4
SEQ = 4096
DEPTH = 1
DEC_BATCH = 16
DEC_SEQ = 2048
PAST_LEN = 128

MLA_HEADS = 8
MLA_NOPE = 128
MLA_ROPE = 64
MLA_QK = MLA_NOPE + MLA_ROPE
MLA_V = 128
Q_LORA = 512
KV_LORA = 256
Q_BLOCK = 128

RET_HEADS = 8
RET_DK = 128
RET_DV = 128
RET_CHUNK = 128

MIX_WIDTH = MLA_HEADS * MLA_V + RET_HEADS * RET_DV
SPLIT_SIZES = (Q_LORA, KV_LORA, MLA_ROPE, RET_HEADS * RET_DK, RET_HEADS * RET_DK, RET_HEADS * RET_DV, RET_HEADS * RET_DV)
IN_WIDTH = Q_LORA + KV_LORA + MLA_ROPE + 2 * RET_HEADS * RET_DK + 2 * RET_HEADS * RET_DV

PEER_HEADS = 8
PEER_NKEYS = 128
PEER_EXPERTS = PEER_NKEYS * PEER_NKEYS
PEER_DKEY = 128
PEER_TOPK = 16
PEER_BLOCK = 128

ROPE_BASE = 10000.0
EPS = 1e-6

kernel_name = "hybrid_mla_retention_peer_adaln_encoder"


def _rms_norm(x, w):
    xf = x.astype(jnp.float32)
    y = xf * lax.rsqrt(jnp.mean(xf * xf, axis=-1, keepdims=True) + EPS)
    return (y * w.astype(jnp.float32)).astype(x.dtype)


def _rope_tables(seq_len, dim, dtype):
    inv = 1.0 / (ROPE_BASE ** (jnp.arange(0, dim, 2, dtype=jnp.float32) / dim))
    ang = jnp.arange(seq_len, dtype=jnp.float32)[:, None] * inv[None, :]
    return jnp.cos(ang).astype(dtype), jnp.sin(ang).astype(dtype)


def _apply_rope(x, cos, sin):
    half = x.shape[-1] // 2
    x1, x2 = x[..., :half], x[..., half:]
    return jnp.concatenate([x1 * cos - x2 * sin, x1 * sin + x2 * cos], axis=-1)


def _mla(c_q, c_kv, k_rope, q_a_norm, kv_a_norm, w_uq, w_uk, w_uv, q_norm, k_norm):
    B, S, _ = c_q.shape
    H = MLA_HEADS
    q = (_rms_norm(c_q, q_a_norm) @ w_uq).reshape(B, S, H, MLA_QK).transpose(0, 2, 1, 3)
    ckv = _rms_norm(c_kv, kv_a_norm)
    k_nope = (ckv @ w_uk).reshape(B, S, H, MLA_NOPE).transpose(0, 2, 1, 3)
    v = (ckv @ w_uv).reshape(B, S, H, MLA_V).transpose(0, 2, 1, 3)
    k_r = jnp.broadcast_to(k_rope[:, None], (B, H, S, MLA_ROPE))
    k = jnp.concatenate([k_nope, k_r], axis=-1)
    q = _rms_norm(q, q_norm)
    k = _rms_norm(k, k_norm)
    cos, sin = _rope_tables(S, MLA_ROPE, q.dtype)
    q = jnp.concatenate([q[..., :MLA_NOPE], _apply_rope(q[..., MLA_NOPE:], cos, sin)], axis=-1)
    k = jnp.concatenate([k[..., :MLA_NOPE], _apply_rope(k[..., MLA_NOPE:], cos, sin)], axis=-1)
    scale = MLA_QK ** -0.5
    nq = S // Q_BLOCK
    qb = q.reshape(B, H, nq, Q_BLOCK, MLA_QK).transpose(2, 0, 1, 3, 4)

    def attend(q_blk):
        s = jnp.einsum('bhqd,bhkd->bhqk', q_blk, k).astype(jnp.float32) * scale
        p = jax.nn.softmax(s, axis=-1).astype(v.dtype)
        return jnp.einsum('bhqk,bhkd->bhqd', p, v)

    o = lax.map(attend, qb)
    return o.transpose(1, 0, 3, 2, 4).reshape(B, S, H * MLA_V)


def _retention_direction(q, k, v, log_gamma, include_diag):
    C = q.shape[3]
    pos = jnp.arange(C, dtype=jnp.float32)
    rel = pos[:, None] - pos[None, :]
    mask = (rel >= 0) if include_diag else (rel > 0)
    decay = jnp.where(mask[None], jnp.exp(log_gamma[:, None, None] * jnp.maximum(rel, 0.0)[None]), 0.0)
    scores = jnp.einsum('bhncd,bhnjd->bhncj', q, k) * decay[:, None].astype(q.dtype)
    intra = jnp.einsum('bhncj,bhnje->bhnce', scores, v)
    k_decay = jnp.exp(log_gamma[:, None] * (C - 1.0 - pos)[None]).astype(k.dtype)
    kv = jnp.einsum('bhnjd,bhnje->bhnde', k * k_decay[:, None, :, None], v).astype(jnp.float32)
    chunk_decay = jnp.exp(log_gamma * C)

    def step(state, kv_c):
        return state * chunk_decay[None, :, None, None] + kv_c, state

    B, H = q.shape[0], q.shape[1]
    init = jnp.zeros((B, H, q.shape[-1], v.shape[-1]), jnp.float32)
    _, prev = lax.scan(step, init, kv.transpose(2, 0, 1, 3, 4))
    prev = prev.transpose(1, 2, 0, 3, 4).astype(q.dtype)
    q_decay = jnp.exp(log_gamma[:, None] * (pos + 1.0)[None]).astype(q.dtype)
    cross = jnp.einsum('bhncd,bhnde->bhnce', q * q_decay[:, None, :, None], prev)
    return intra + cross


def _retention(rq, rk, rv, rg, decay_logit, gn_w):
    B, S, _ = rq.shape
    H = RET_HEADS
    nc = S // RET_CHUNK

    def heads(t, d):
        return t.reshape(B, S, H, d).transpose(0, 2, 1, 3)

    cos, sin = _rope_tables(S, RET_DK, rq.dtype)
    q = _apply_rope(heads(rq, RET_DK), cos, sin)
    k = _apply_rope(heads(rk, RET_DK), cos, sin) * (RET_DK ** -0.5)
    v = heads(rv, RET_DV)

    def chunk(t):
        return t.reshape(B, H, nc, RET_CHUNK, t.shape[-1])

    def flip(t):
        return jnp.flip(t, axis=2)

    log_g = jax.nn.log_sigmoid(decay_logit.astype(jnp.float32))
    fwd = _retention_direction(chunk(q), chunk(k), chunk(v), log_g[0], True)
    bwd = _retention_direction(chunk(flip(q)), chunk(flip(k)), chunk(flip(v)), log_g[1], False)
    o = fwd.reshape(B, H, S, RET_DV) + flip(bwd.reshape(B, H, S, RET_DV))
    of = o.astype(jnp.float32)
    mu = jnp.mean(of, axis=-1, keepdims=True)
    var = jnp.mean(jnp.square(of - mu), axis=-1, keepdims=True)
    on = (of - mu) * lax.rsqrt(var + EPS)
    on = on.transpose(0, 2, 1, 3).reshape(B, S, H * RET_DV) * gn_w.astype(jnp.float32)
    return jax.nn.silu(rg) * on.astype(rg.dtype)


def _peer(h, peer_wq, peer_sub_keys, peer_u, peer_v):
    B, S, D = h.shape
    T = B * S
    ht = h.reshape(T, D)
    q = (ht @ peer_wq).reshape(T, PEER_HEADS, 2, PEER_DKEY)
    s = jnp.einsum('thpd,hpkd->thpk', q, peer_sub_keys).astype(jnp.float32)
    s1, i1 = lax.top_k(s[:, :, 0], PEER_TOPK)
    s2, i2 = lax.top_k(s[:, :, 1], PEER_TOPK)
    cand_s = (s1[..., :, None] + s2[..., None, :]).reshape(T, PEER_HEADS, PEER_TOPK * PEER_TOPK)
    cand_i = (i1[..., :, None] * PEER_NKEYS + i2[..., None, :]).reshape(T, PEER_HEADS, PEER_TOPK * PEER_TOPK)
    top_s, sel = lax.top_k(cand_s, PEER_TOPK)
    idx = jnp.take_along_axis(cand_i, sel, axis=-1)
    g = jax.nn.softmax(top_s, axis=-1).astype(h.dtype)
    nb = T // PEER_BLOCK

    def block(args):
        hb, ib, gb = args
        u = jnp.take(peer_u, ib, axis=0)
        a = jnp.einsum('td,thkd->thk', hb, u)
        w = gb * jax.nn.gelu(a)
        vs = jnp.take(peer_v, ib, axis=0)
        return jnp.einsum('thk,thkd->td', w, vs)

    out = lax.map(block, (ht.reshape(nb, PEER_BLOCK, D),
                          idx.reshape(nb, PEER_BLOCK, PEER_HEADS, PEER_TOPK),
                          g.reshape(nb, PEER_BLOCK, PEER_HEADS, PEER_TOPK)))
    return out.reshape(B, S, D)


def _layer(x, c, norm1_w, norm2_w, w_ada, b_ada, w_in, q_a_norm, kv_a_norm, w_uq, w_uk, w_uv,
           q_norm, k_norm, ret_decay_logit, ret_gn_w, w_o, peer_wq, peer_sub_keys, peer_u, peer_v):
    mod = (jax.nn.silu(c) @ w_ada + b_ada)[:, None, :]
    shift1, scale1, gate1, shift2, scale2, gate2 = jnp.split(mod, 6, axis=-1)
    h = _rms_norm(x, norm1_w) * (1 + scale1) + shift1
    proj = h @ w_in
    offs = np.cumsum(np.array(SPLIT_SIZES))[:-1].tolist()
    c_q, c_kv, k_rope, rq, rk, rv, rg = jnp.split(proj, offs, axis=-1)
    a_out = _mla(c_q, c_kv, k_rope, q_a_norm, kv_a_norm, w_uq, w_uk, w_uv, q_norm, k_norm)
    r_out = _retention(rq, rk, rv, rg, ret_decay_logit, ret_gn_w)
    mix = jnp.concatenate([a_out, r_out], axis=-1) @ w_o
    x = x + gate1 * mix
    h2 = _rms_norm(x, norm2_w) * (1 + scale2) + shift2
    x = x + gate2 * _peer(h2, peer_wq, peer_sub_keys, peer_u, peer_v)
    return x


def _trunk(x, c, params):
    for l in range(DEPTH):
        x = _layer(x, c, *[p[l] for p in params])
    return x


def setup_inputs(seed: int = 0) -> dict:
    key = jax.random.key(seed)
    ks = jax.random.split(key, 24)
    f32 = jnp.float32
    D = D_MODEL

    def nrm(k, shape, scale):
        return jax.random.normal(k, shape, f32) * scale

    g = 1.0 - 2.0 ** (-5.0 - jnp.arange(RET_HEADS, dtype=f32))
    base_logit = jnp.log(g) - jnp.log1p(-g)
    return {
        "x_prompt": nrm(ks[0], (BATCH, SEQ, D), 1.0),
        "x_sample": nrm(ks[1], (DEC_BATCH, DEC_SEQ, D), 1.0),
        "c_prompt": nrm(ks[2], (BATCH, D), 1.0),
        "c_sample": nrm(ks[3], (DEC_BATCH, D), 1.0),
        "norm1_w": 1.0 + nrm(ks[4], (DEPTH, D), 0.01),
        "norm2_w": 1.0 + nrm(ks[5], (DEPTH, D), 0.01),
        "w_ada": nrm(ks[6], (DEPTH, D, 6 * D), 0.5 * D ** -0.5),
        "b_ada": nrm(ks[7], (DEPTH, 6 * D), 0.01),
        "w_in": nrm(ks[8], (DEPTH, D, IN_WIDTH), D ** -0.5),
        "q_a_norm": 1.0 + nrm(ks[9], (DEPTH, Q_LORA), 0.01),
        "kv_a_norm": 1.0 + nrm(ks[10], (DEPTH, KV_LORA), 0.01),
        "w_uq": nrm(ks[11], (DEPTH, Q_LORA, MLA_HEADS * MLA_QK), Q_LORA ** -0.5),
        "w_uk": nrm(ks[12], (DEPTH, KV_LORA, MLA_HEADS * MLA_NOPE), KV_LORA ** -0.5),
        "w_uv": nrm(ks[13], (DEPTH, KV_LORA, MLA_HEADS * MLA_V), KV_LORA ** -0.5),
        "q_norm": 1.0 + nrm(ks[14], (DEPTH, MLA_QK), 0.01),
        "k_norm": 1.0 + nrm(ks[15], (DEPTH, MLA_QK), 0.01),
        "ret_decay_logit": jnp.broadcast_to(base_logit, (DEPTH, 2, RET_HEADS)) + nrm(ks[16], (DEPTH, 2, RET_HEADS), 0.01),
        "ret_gn_w": 1.0 + nrm(ks[17], (DEPTH, RET_HEADS * RET_DV), 0.01),
        "w_o": nrm(ks[18], (DEPTH, MIX_WIDTH, D), MIX_WIDTH ** -0.5),
        "peer_wq": nrm(ks[19], (DEPTH, D, PEER_HEADS * 2 * PEER_DKEY), D ** -0.5),
        "peer_sub_keys": nrm(ks[20], (DEPTH, PEER_HEADS, 2, PEER_NKEYS, PEER_DKEY), PEER_DKEY ** -0.5),
        "peer_u": nrm(ks[21], (DEPTH, PEER_EXPERTS, D), D ** -0.5),
        "peer_v": nrm(ks[22], (DEPTH, PEER_EXPERTS, D), 0.25),
    }


def reference(x_prompt, x_sample, c_prompt, c_sample, norm1_w, norm2_w, w_ada, b_ada, w_in,
              q_a_norm, kv_a_norm, w_uq, w_uk, w_uv, q_norm, k_norm, ret_decay_logit, ret_gn_w,
              w_o, peer_wq, peer_sub_keys, peer_u, peer_v):
    params = (norm1_w, norm2_w, w_ada, b_ada, w_in, q_a_norm, kv_a_norm, w_uq, w_uk, w_uv,
              q_norm, k_norm, ret_decay_logit, ret_gn_w, w_o, peer_wq, peer_sub_keys, peer_u, peer_v)
    y_prompt = _trunk(x_prompt, c_prompt, params)
    y_sample = _trunk(x_sample, c_sample, params)
    return (y_prompt, y_sample)
```

```python
import functools
import math

import jax
import jax.numpy as jnp
from jax import lax
from jax.experimental import pallas as pl
from jax.experimental.pallas import tpu as pltpu

F32 = jnp.float32
BF16 = jnp.bfloat16

MLA_HEADS = 8
MLA_NOPE = 128
MLA_ROPE = 64
MLA_QK = MLA_NOPE + MLA_ROPE
MLA_V = 128
Q_LORA = 512
KV_LORA = 256
RET_HEADS = 8
RET_DK = 128
RET_DV = 128
RET_CHUNK = 128
PEER_HEADS = 8
PEER_NKEYS = 128
PEER_DKEY = 128
PEER_TOPK = 16
ROPE_BASE = 10000.0
EPS = 1e-6

LANE = 128
HEAD_PAD = 2 * LANE
NOT_RANKED = 99.0
VMEM_LIMIT = 56 * 1024 * 1024

TM_PREMIX = 256
TQ_ATTN = 256
TM_POSTMIX = 256
TM_ROUTE = 256
TM_PEER = 512
EB_PEER = 1024
EG_PEER = 256


def _params(sem):
    return pltpu.CompilerParams(dimension_semantics=sem, vmem_limit_bytes=VMEM_LIMIT)


def _const_spec(shape):
    nd = len(shape)
    return pl.BlockSpec(shape, lambda *_: (0,) * nd, pipeline_mode=pl.Buffered(1))


def _adaln_kernel(c_ref, w_ref, b_ref, o_ref):
    c = c_ref[...]
    sc = c / (1.0 + jnp.exp(-c))
    o_ref[...] = jnp.dot(sc, w_ref[...], preferred_element_type=F32,
                         precision=lax.Precision.HIGHEST) + b_ref[...]


def _adaln(c, w_ada, b_ada):
    nb, d = c.shape
    n = w_ada.shape[1]
    tn = math.gcd(n, 1024)
    return pl.pallas_call(
        _adaln_kernel,
        grid=(n // tn,),
        in_specs=[pl.BlockSpec((nb, d), lambda j: (0, 0)),
                  pl.BlockSpec((d, tn), lambda j: (0, j)),
                  pl.BlockSpec((1, tn), lambda j: (0, j))],
        out_specs=pl.BlockSpec((nb, tn), lambda j: (0, j)),
        out_shape=jax.ShapeDtypeStruct((nb, n), F32),
        compiler_params=_params(("arbitrary",)),
        name="adaln",
    )(c, w_ada, b_ada.reshape(1, n))


def _rms(x, w):
    ms = jnp.mean(x * x, axis=-1, keepdims=True)
    return x * lax.rsqrt(ms + EPS) * w


def _rot(x, c, s):
    return x * c + pltpu.roll(x, 64, 1) * s


def _premix_kernel(x_ref, mod_ref, n1_ref, win_ref, qan_ref, kvan_ref, wuq_ref, qn_ref,
                   wuk_ref, wuv_ref, kn_ref, cm_ref, sm_ref, cr_ref, sr_ref,
                   q_ref, k_ref, v_ref, rq_ref, rk_ref, rv_ref, rg_ref):
    x = x_ref[0]
    shift1 = mod_ref[0, 0:1, :]
    scale1 = mod_ref[0, 1:2, :]
    h = (_rms(x, n1_ref[...]) * (1.0 + scale1) + shift1).astype(BF16)

    o_ckv = Q_LORA
    o_kr = o_ckv + KV_LORA
    o_rq = o_kr + LANE
    hk = RET_HEADS * RET_DK
    hv = RET_HEADS * RET_DV
    o_rk = o_rq + hk
    o_rv = o_rk + hk
    o_rg = o_rv + hv

    def proj(a, b):
        return jnp.dot(h, win_ref[:, a:b], preferred_element_type=F32)

    cm = cm_ref[...]
    sm = sm_ref[...]
    att_scale = MLA_QK ** -0.5

    cq = _rms(proj(0, o_ckv), qan_ref[...]).astype(BF16)
    q_raw = jnp.dot(cq, wuq_ref[...], preferred_element_type=F32)
    qn = qn_ref[...]
    for hd in range(MLA_HEADS):
        qh = q_raw[:, hd * HEAD_PAD:(hd + 1) * HEAD_PAD]
        r = lax.rsqrt(jnp.sum(qh * qh, axis=-1, keepdims=True) * (1.0 / MLA_QK) + EPS)
        qh = qh * (r * att_scale) * qn
        q_ref[0, :, hd * HEAD_PAD:hd * HEAD_PAD + LANE] = qh[:, :LANE].astype(BF16)
        q_ref[0, :, hd * HEAD_PAD + LANE:(hd + 1) * HEAD_PAD] = _rot(qh[:, LANE:], cm, sm).astype(BF16)

    ckv = _rms(proj(o_ckv, o_kr), kvan_ref[...]).astype(BF16)
    k_nope = jnp.dot(ckv, wuk_ref[...], preferred_element_type=F32)
    v_ref[0] = jnp.dot(ckv, wuv_ref[...], preferred_element_type=F32).astype(BF16)
    kr = proj(o_kr, o_rq)
    ss_r = jnp.sum(kr * kr, axis=-1, keepdims=True)
    kn = kn_ref[...]
    kr_rot = _rot(kr * kn[:, LANE:], cm, sm)
    for hd in range(MLA_HEADS):
        kh = k_nope[:, hd * MLA_NOPE:(hd + 1) * MLA_NOPE]
        r = lax.rsqrt((jnp.sum(kh * kh, axis=-1, keepdims=True) + ss_r) * (1.0 / MLA_QK) + EPS)
        k_ref[0, :, hd * HEAD_PAD:hd * HEAD_PAD + LANE] = (kh * r * kn[:, :LANE]).astype(BF16)
        k_ref[0, :, hd * HEAD_PAD + LANE:(hd + 1) * HEAD_PAD] = (kr_rot * r).astype(BF16)

    cr = cr_ref[...]
    sr = sr_ref[...]
    rq = proj(o_rq, o_rk)
    rk = proj(o_rk, o_rv)
    k_scale = RET_DK ** -0.5
    for hd in range(RET_HEADS):
        sl = slice(hd * RET_DK, (hd + 1) * RET_DK)
        rq_ref[0, :, sl] = _rot(rq[:, sl], cr, sr).astype(BF16)
        rk_ref[0, :, sl] = (_rot(rk[:, sl], cr, sr) * k_scale).astype(BF16)
    rv_ref[0] = proj(o_rv, o_rg).astype(BF16)
    rg_ref[0] = proj(o_rg, o_rg + hv).astype(BF16)


def _premix(x, mod, w):
    b, s, d = x.shape
    tm = min(TM_PREMIX, s)
    tok = lambda width: pl.BlockSpec((1, tm, width), lambda i, j: (i, j, 0))
    tab = pl.BlockSpec((tm, LANE), lambda i, j: (j, 0))
    cm, sm, cr, sr = _rope_tables(s)
    hq = MLA_HEADS * HEAD_PAD
    hv = MLA_HEADS * MLA_V
    rk = RET_HEADS * RET_DK
    rv = RET_HEADS * RET_DV
    out_widths = (hq, hq, hv, rk, rk, rv, rv)
    return pl.pallas_call(
        _premix_kernel,
        grid=(b, s // tm),
        in_specs=[tok(d), pl.BlockSpec((1, 6, d), lambda i, j: (i, 0, 0)),
                  _const_spec((1, d)), _const_spec(w["w_in"].shape),
                  _const_spec((1, Q_LORA)), _const_spec((1, KV_LORA)),
                  _const_spec(w["w_uq"].shape), _const_spec((1, HEAD_PAD)),
                  _const_spec(w["w_uk"].shape), _const_spec(w["w_uv"].shape),
                  _const_spec((1, HEAD_PAD)), tab, tab, tab, tab],
        out_specs=[tok(n) for n in out_widths],
        out_shape=[jax.ShapeDtypeStruct((b, s, n), BF16) for n in out_widths],
        compiler_params=_params(("parallel", "parallel")),
        name="premix",
    )(x, mod, w["norm1"], w["w_in"], w["q_a_norm"], w["kv_a_norm"], w["w_uq"], w["q_norm"],
      w["w_uk"], w["w_uv"], w["k_norm"], cm, sm, cr, sr)


def _rope_tables(s):
    pos = jnp.arange(s, dtype=F32)[:, None]

    def cs(dim):
        inv = 1.0 / (ROPE_BASE ** (jnp.arange(0, dim, 2, dtype=F32) / dim))
        ang = pos * inv[None, :]
        return jnp.cos(ang), jnp.sin(ang)

    c, sn = cs(MLA_ROPE)
    z = jnp.zeros_like(c)
    cm = jnp.concatenate([c, z, c, z], axis=-1)
    sm = jnp.concatenate([-sn, z, sn, z], axis=-1)
    c, sn = cs(RET_DK)
    cr = jnp.concatenate([c, c], axis=-1)
    sr = jnp.concatenate([-sn, sn], axis=-1)
    return cm, sm, cr, sr


def _attn_kernel(q_ref, k_ref, v_ref, o_ref):
    s = lax.dot_general(q_ref[0], k_ref[0], (((1,), (1,)), ((), ())), preferred_element_type=F32)
    m = jnp.max(s, axis=-1, keepdims=True)
    p = jnp.exp(s - m)
    l = jnp.sum(p, axis=-1, keepdims=True)
    o = jnp.dot(p.astype(BF16), v_ref[0], preferred_element_type=F32)
    o_ref[0] = (o / l).astype(BF16)


def _attention(q, k, v):
    b, s, _ = q.shape
    tq = min(TQ_ATTN, s)
    return pl.pallas_call(
        _attn_kernel,
        grid=(b, MLA_HEADS, s // tq),
        in_specs=[pl.BlockSpec((1, tq, HEAD_PAD), lambda i, h, j: (i, j, h)),
                  pl.BlockSpec((1, s, HEAD_PAD), lambda i, h, j: (i, 0, h)),
                  pl.BlockSpec((1, s, MLA_V), lambda i, h, j: (i, 0, h))],
        out_specs=pl.BlockSpec((1, tq, MLA_V), lambda i, h, j: (i, j, h)),
        out_shape=jax.ShapeDtypeStruct((b, s, MLA_HEADS * MLA_V), BF16),
        compiler_params=_params(("parallel", "parallel", "arbitrary")),
        name="attention",
    )(q, k, v)


def _retention_kernel(lg_ref, q_ref, k_ref, v_ref, g_ref, gn_ref, o_ref, cb_ref):
    c = RET_CHUNK
    nc = q_ref.shape[1] // c
    logit = lg_ref[0]
    lg = jnp.minimum(logit, 0.0) - jnp.log1p(jnp.exp(-jnp.abs(logit)))
    lg_f = lg[0:1, :]
    lg_b = lg[1:2, :]
    row = lax.broadcasted_iota(jnp.int32, (c, c), 0).astype(F32)
    col = lax.broadcasted_iota(jnp.int32, (c, c), 1).astype(F32)
    rel = row - col
    decay = jnp.where(rel >= 0, jnp.exp(lg_f * jnp.maximum(rel, 0.0)),
                      jnp.exp(lg_b * jnp.maximum(-rel, 0.0)))
    pos = lax.broadcasted_iota(jnp.int32, (c, 1), 0).astype(F32)
    qd_f = jnp.exp(lg_f * (pos + 1.0))
    kd_f = jnp.exp(lg_f * (c - 1.0 - pos))
    qd_b = jnp.exp(lg_b * (c - pos))
    kd_b = jnp.exp(lg_b * pos)
    cd_f = jnp.exp(lg_f * float(c))
    cd_b = jnp.exp(lg_b * float(c))
    gn = gn_ref[...]

    def chunk(ref, n):
        return ref[0, pl.ds(pl.multiple_of(n * c, c), c), :]

    def kv_update(state, kn, vn, kd, cd):
        kt = (kn.astype(F32) * kd).T.astype(BF16)
        return state * cd + jnp.dot(kt, vn, preferred_element_type=F32)

    def bwd_step(i, state):
        n = nc - 1 - i
        qn = chunk(q_ref, n)
        cross = jnp.dot((qn.astype(F32) * qd_b).astype(BF16), state.astype(BF16),
                        preferred_element_type=F32)
        cb_ref[pl.ds(pl.multiple_of(n * c, c), c), :] = cross
        return kv_update(state, chunk(k_ref, n), chunk(v_ref, n), kd_b, cd_b)

    zero = jnp.zeros((RET_DK, RET_DV), F32)
    lax.fori_loop(0, nc, bwd_step, zero)

    def fwd_step(n, state):
        qn = chunk(q_ref, n)
        kn = chunk(k_ref, n)
        vn = chunk(v_ref, n)
        sc = lax.dot_general(qn, kn, (((1,), (1,)), ((), ())), preferred_element_type=F32) * decay
        o = jnp.dot(sc.astype(BF16), vn, preferred_element_type=F32)
        o = o + jnp.dot((qn.astype(F32) * qd_f).astype(BF16), state.astype(BF16),
                        preferred_element_type=F32)
        o = o + cb_ref[pl.ds(pl.multiple_of(n * c, c), c), :]
        mu = jnp.mean(o, axis=-1, keepdims=True)
        d = o - mu
        var = jnp.mean(d * d, axis=-1, keepdims=True)
        on = d * lax.rsqrt(var + EPS) * gn
        g = chunk(g_ref, n).astype(F32)
        o_ref[0, pl.ds(pl.multiple_of(n * c, c), c), :] = (g / (1.0 + jnp.exp(-g)) * on).astype(BF16)
        return kv_update(state, kn, vn, kd_f, cd_f)

    lax.fori_loop(0, nc, fwd_step, zero)


def _retention(rq, rk, rv, rg, w):
    b, s, _ = rq.shape
    hs = lambda width: pl.BlockSpec((1, s, width), lambda i, h: (i, 0, h))
    return pl.pallas_call(
        _retention_kernel,
        grid=(b, RET_HEADS),
        in_specs=[pl.BlockSpec((1, 2, 1), lambda i, h: (h, 0, 0)),
                  hs(RET_DK), hs(RET_DK), hs(RET_DV), hs(RET_DV),
                  pl.BlockSpec((1, RET_DV), lambda i, h: (0, h))],
        out_specs=hs(RET_DV),
        out_shape=jax.ShapeDtypeStruct((b, s, RET_HEADS * RET_DV), BF16),
        scratch_shapes=[pltpu.VMEM((s, RET_DV), F32)],
        compiler_params=_params(("parallel", "parallel")),
        name="retention",
    )(w["ret_logit"], rq, rk, rv, rg, w["ret_gn"])


def _postmix_kernel(x_ref, a_ref, r_ref, mod_ref, wo_ref, n2_ref, x1_ref, h2_ref):
    na = a_ref.shape[2]
    mix = jnp.dot(a_ref[0], wo_ref[:na, :], preferred_element_type=F32)
    mix = mix + jnp.dot(r_ref[0], wo_ref[na:, :], preferred_element_type=F32)
    x1 = x_ref[0] + mod_ref[0, 2:3, :] * mix
    x1_ref[0] = x1
    h2 = _rms(x1, n2_ref[...]) * (1.0 + mod_ref[0, 4:5, :]) + mod_ref[0, 3:4, :]
    h2_ref[0] = h2.astype(BF16)


def _postmix(x, a, r, mod, w):
    b, s, d = x.shape
    tm = min(TM_POSTMIX, s)
    tok = lambda width: pl.BlockSpec((1, tm, width), lambda i, j: (i, j, 0))
    return pl.pallas_call(
        _postmix_kernel,
        grid=(b, s // tm),
        in_specs=[tok(d), tok(a.shape[2]), tok(r.shape[2]),
                  pl.BlockSpec((1, 6, d), lambda i, j: (i, 0, 0)),
                  _const_spec(w["w_o"].shape), _const_spec((1, d))],
        out_specs=[tok(d), tok(d)],
        out_shape=[jax.ShapeDtypeStruct((b, s, d), F32), jax.ShapeDtypeStruct((b, s, d), BF16)],
        compiler_params=_params(("parallel", "parallel")),
        name="postmix",
    )(x, a, r, mod, w["w_o"], w["norm2"])


def _topk_sorted(s, kiota, vals_ref):
    nk = s.shape[0]

    def body(r, carry):
        rem, rank = carry
        m = jnp.max(rem, axis=0, keepdims=True)
        idx = jnp.min(jnp.where(rem == m, kiota, float(nk)), axis=0, keepdims=True)
        hit = kiota == idx
        vals_ref[pl.ds(r, 1), :] = m
        return jnp.where(hit, -jnp.inf, rem), jnp.where(hit, r.astype(F32), rank)

    _, rank = lax.fori_loop(0, PEER_TOPK, body, (s, jnp.full(s.shape, NOT_RANKED, F32)))
    return rank


def _routing_kernel(h2_ref, wq_ref, keys_ref, rta_ref, rtb_ref, pq_ref, a_ref, b_ref, cand_ref, cnt_ref):
    k = PEER_TOPK
    hd = pl.program_id(2)
    tm = h2_ref.shape[1]

    @pl.when(hd == 0)
    def _():
        pq = jnp.dot(h2_ref[0], wq_ref[...], preferred_element_type=F32).astype(BF16)
        for i in range(2 * PEER_HEADS):
            pq_ref[i] = pq[:, i * PEER_DKEY:(i + 1) * PEER_DKEY]

    nt = (((1,), (1,)), ((), ()))
    s1 = lax.dot_general(keys_ref[0, 0], pq_ref[2 * hd], nt, preferred_element_type=F32)
    s2 = lax.dot_general(keys_ref[0, 1], pq_ref[2 * hd + 1], nt, preferred_element_type=F32)
    kiota = lax.broadcasted_iota(jnp.int32, (PEER_NKEYS, tm), 0).astype(F32)
    rank1 = _topk_sorted(s1, kiota, a_ref)
    rank2 = _topk_sorted(s2, kiota, b_ref)

    b16 = b_ref[...]
    for r1 in range(k):
        cand_ref[r1 * k:(r1 + 1) * k, :] = a_ref[r1:r1 + 1, :] + b16
    fidx = lax.broadcasted_iota(jnp.int32, (k * k, tm), 0).astype(F32)
    mx = a_ref[0:1, :] + b_ref[0:1, :]

    def pick(_, z):
        cand = cand_ref[...]
        m = jnp.max(cand, axis=0, keepdims=True)
        sel = jnp.min(jnp.where(cand == m, fidx, float(k * k)), axis=0, keepdims=True)
        cand_ref[...] = jnp.where(fidx == sel, -jnp.inf, cand)
        return z + jnp.exp(m - mx)

    z = lax.fori_loop(0, k, pick, jnp.zeros((1, tm), F32))
    picked = jnp.where(cand_ref[...] == -jnp.inf, 1.0, 0.0)
    for r1 in range(k):
        cnt_ref[r1:r1 + 1, :] = jnp.sum(picked[r1 * k:(r1 + 1) * k, :], axis=0, keepdims=True)

    def spread(r1, n1):
        return n1 + jnp.where(rank1 == r1.astype(F32), cnt_ref[pl.ds(r1, 1), :], 0.0)

    n1 = lax.fori_loop(0, k, spread, jnp.zeros((PEER_NKEYS, tm), F32))
    rta_ref[0, 0, 0] = jnp.where(rank1 < float(k), jnp.exp(s1 - a_ref[0:1, :]), 0.0) / z
    rta_ref[0, 0, 1] = n1
    rtb_ref[0, 0, 0] = rank2.astype(BF16)
    rtb_ref[0, 0, 1] = jnp.exp(s2 - b_ref[0:1, :]).astype(BF16)


def _routing(h2, w):
    b, s, d = h2.shape
    tm = min(TM_ROUTE, s)
    k = PEER_TOPK
    out = lambda: pl.BlockSpec((1, 1, 2, PEER_NKEYS, tm), lambda i, j, h: (i, h, 0, 0, j))
    shape = (b, PEER_HEADS, 2, PEER_NKEYS, s)
    return pl.pallas_call(
        _routing_kernel,
        grid=(b, s // tm, PEER_HEADS),
        in_specs=[pl.BlockSpec((1, tm, d), lambda i, j, h: (i, j, 0)),
                  _const_spec(w["peer_wq"].shape),
                  pl.BlockSpec((1, 2, PEER_NKEYS, PEER_DKEY), lambda i, j, h: (h, 0, 0, 0))],
        out_specs=[out(), out()],
        out_shape=[jax.ShapeDtypeStruct(shape, F32), jax.ShapeDtypeStruct(shape, BF16)],
        scratch_shapes=[pltpu.VMEM((2 * PEER_HEADS, tm, PEER_DKEY), BF16),
                        pltpu.VMEM((k, tm), F32), pltpu.VMEM((k, tm), F32),
                        pltpu.VMEM((k * k, tm), F32), pltpu.VMEM((k, tm), F32)],
        compiler_params=_params(("parallel", "parallel", "arbitrary")),
        name="peer_routing",
    )(h2, w["peer_wq"], w["peer_keys"])


def _gelu_tanh(x):
    return 0.5 * x * (1.0 + jnp.tanh(math.sqrt(2.0 / math.pi) * (x + 0.044715 * (x * x * x))))


def _peer_kernel(h2_ref, u_ref, vt_ref, rta_ref, rtb_ref, x1_ref, mod_ref, y_ref, acc_ref):
    e = pl.program_id(2)
    ne = pl.num_programs(2)
    keys_per_step = EB_PEER // PEER_NKEYS
    keys_per_group = EG_PEER // PEER_NKEYS

    @pl.when(e == 0)
    def _():
        acc_ref[...] = jnp.zeros_like(acc_ref)

    h2 = h2_ref[0]
    row0 = pl.multiple_of(e * keys_per_step, keys_per_step)
    for g in range(EB_PEER // EG_PEER):
        at = lax.dot_general(u_ref[g * EG_PEER:(g + 1) * EG_PEER, :], h2, (((1,), (1,)), ((), ())),
                             preferred_element_type=F32)
        ys = []
        for j in range(keys_per_group):
            c = g * keys_per_group + j
            a = at[j * PEER_NKEYS:(j + 1) * PEER_NKEYS, :]
            wgt = jnp.zeros_like(a)
            for hd in range(PEER_HEADS):
                p1 = rta_ref[0, hd, 0, pl.ds(row0, keys_per_step), :][c:c + 1, :]
                n1 = rta_ref[0, hd, 1, pl.ds(row0, keys_per_step), :][c:c + 1, :]
                r2 = rtb_ref[0, hd, 0].astype(F32)
                p2 = rtb_ref[0, hd, 1].astype(F32)
                wgt = wgt + jnp.where(r2 < n1, p1 * p2, 0.0)
            ys.append((wgt * _gelu_tanh(a)).astype(BF16))
        y = jnp.concatenate(ys, axis=0)
        acc_ref[...] += jnp.dot(vt_ref[:, g * EG_PEER:(g + 1) * EG_PEER], y, preferred_element_type=F32)

    @pl.when(e == ne - 1)
    def _():
        y_ref[0] = x1_ref[0] + mod_ref[0, 5:6, :] * acc_ref[...].T


def _peer(h2, rta, rtb, x1, mod, w):
    b, s, d = h2.shape
    tm = min(TM_PEER, s)
    assert (EB_PEER // PEER_NKEYS) % 8 == 0 and EB_PEER % EG_PEER == 0 and EG_PEER % PEER_NKEYS == 0
    ne = w["peer_u"].shape[0] // EB_PEER
    once = pl.Buffered(1)
    tok = lambda **kw: pl.BlockSpec((1, tm, d), lambda i, j, e: (i, j, 0), **kw)
    rt = lambda: pl.BlockSpec((1, PEER_HEADS, 2, PEER_NKEYS, tm), lambda i, j, e: (i, 0, 0, 0, j),
                              pipeline_mode=once)
    return pl.pallas_call(
        _peer_kernel,
        grid=(b, s // tm, ne),
        in_specs=[tok(pipeline_mode=once),
                  pl.BlockSpec((EB_PEER, d), lambda i, j, e: (e, 0)),
                  pl.BlockSpec((d, EB_PEER), lambda i, j, e: (0, e)),
                  rt(), rt(),
                  tok(pipeline_mode=once),
                  pl.BlockSpec((1, 6, d), lambda i, j, e: (i, 0, 0))],
        out_specs=tok(),
        out_shape=jax.ShapeDtypeStruct((b, s, d), F32),
        scratch_shapes=[pltpu.VMEM((d, tm), F32)],
        compiler_params=_params(("parallel", "parallel", "arbitrary")),
        name="peer_experts",
    )(h2, w["peer_u"], w["peer_vt"], rta, rtb, x1, mod)


def _prepare_weights(norm1_w, norm2_w, w_in, q_a_norm, kv_a_norm, w_uq, w_uk, w_uv, q_norm, k_norm,
                     ret_decay_logit, ret_gn_w, w_o, peer_wq, peer_sub_keys, peer_u, peer_v):
    half = MLA_ROPE // 2

    def rope_lanes(t):
        z = jnp.zeros(t.shape[:-1] + (half,), t.dtype)
        return jnp.concatenate([t[..., :half], z, t[..., half:], z], axis=-1)

    def head_lanes(t):
        return jnp.concatenate([t[..., :MLA_NOPE], rope_lanes(t[..., MLA_NOPE:])], axis=-1)

    w_in = w_in[0]
    o = Q_LORA + KV_LORA
    w_in_p = jnp.concatenate([w_in[:, :o], rope_lanes(w_in[:, o:o + MLA_ROPE]), w_in[:, o + MLA_ROPE:]], axis=1)
    w_uq_p = head_lanes(w_uq[0].reshape(Q_LORA, MLA_HEADS, MLA_QK)).reshape(Q_LORA, MLA_HEADS * HEAD_PAD)
    row = lambda t: t.reshape(1, -1).astype(F32)
    return {
        "norm1": row(norm1_w[0]), "norm2": row(norm2_w[0]),
        "w_in": w_in_p.astype(BF16),
        "q_a_norm": row(q_a_norm[0]), "kv_a_norm": row(kv_a_norm[0]),
        "w_uq": w_uq_p.astype(BF16), "w_uk": w_uk[0].astype(BF16), "w_uv": w_uv[0].astype(BF16),
        "q_norm": row(head_lanes(q_norm[0])), "k_norm": row(head_lanes(k_norm[0])),
        "ret_logit": ret_decay_logit[0].T.reshape(RET_HEADS, 2, 1).astype(F32),
        "ret_gn": row(ret_gn_w[0]),
        "w_o": w_o[0].astype(BF16),
        "peer_wq": peer_wq[0].astype(BF16),
        "peer_keys": peer_sub_keys[0].astype(BF16),
        "peer_u": peer_u[0].astype(BF16),
        "peer_vt": peer_v[0].T.astype(BF16),
    }


def _trunk(x, mod, w):
    q, k, v, rq, rk, rv, rg = _premix(x, mod, w)
    a = _attention(q, k, v)
    r = _retention(rq, rk, rv, rg, w)
    x1, h2 = _postmix(x, a, r, mod, w)
    rta, rtb = _routing(h2, w)
    return _peer(h2, rta, rtb, x1, mod, w)


def kernel(x_prompt, x_sample, c_prompt, c_sample, norm1_w, norm2_w, w_ada, b_ada, w_in, q_a_norm, kv_a_norm, w_uq, w_uk, w_uv, q_norm, k_norm, ret_decay_logit, ret_gn_w, w_o, peer_wq, peer_sub_keys, peer_u, peer_v):
    assert w_ada.shape[0] == 1, "single-layer trunk"
    d = x_prompt.shape[-1]
    w = _prepare_weights(norm1_w, norm2_w, w_in, q_a_norm, kv_a_norm, w_uq, w_uk, w_uv, q_norm, k_norm,
                         ret_decay_logit, ret_gn_w, w_o, peer_wq, peer_sub_keys, peer_u, peer_v)
    nbp = c_prompt.shape[0]
    mod = _adaln(jnp.concatenate([c_prompt, c_sample], axis=0), w_ada[0], b_ada[0]).reshape(-1, 6, d)
    y_prompt = _trunk(x_prompt, mod[:nbp], w)
    y_sample = _trunk(x_sample, mod[nbp:], w)
    return (y_prompt, y_sample)
```

```python
import math

import jax
import jax.numpy as jnp
from jax import lax
from jax.experimental import pallas as pl
from jax.experimental.pallas import tpu as pltpu

F32 = jnp.float32
BF16 = jnp.bfloat16

MLA_HEADS = 8
MLA_NOPE = 128
MLA_ROPE = 64
MLA_QK = MLA_NOPE + MLA_ROPE
MLA_V = 128
Q_LORA = 512
KV_LORA = 256
RET_HEADS = 8
RET_DK = 128
RET_DV = 128
RET_CHUNK = 128
PEER_HEADS = 8
PEER_NKEYS = 128
PEER_DKEY = 128
PEER_TOPK = 16
ROPE_BASE = 10000.0
EPS = 1e-6

LANE = 128
HEAD_PAD = 2 * LANE
NOT_RANKED = 99.0
MARK = 2.0 ** 100
NOT_A_CANDIDATE = -(2.0 ** 120)
ORDER_END = 1e9
VMEM_LIMIT = 56 * 1024 * 1024

TM_PREMIX = 256
TQ_ATTN = 256
TM_POSTMIX = 256
TM_ROUTE = 256
TM_PEER = 1024
EB_PEER = 1024
EH_PEER = 512
ACC_ROWS = 512
BF16_ROWS = 16
TC_PEER = 256
TM_FINAL = 512


def _params(sem):
    return pltpu.CompilerParams(dimension_semantics=sem, vmem_limit_bytes=VMEM_LIMIT)


def _const_spec(shape):
    nd = len(shape)
    return pl.BlockSpec(shape, lambda *_: (0,) * nd, pipeline_mode=pl.Buffered(1))


def _adaln_kernel(c_ref, w_ref, b_ref, o_ref):
    c = c_ref[...]
    sc = c / (1.0 + jnp.exp(-c))
    o_ref[...] = jnp.dot(sc, w_ref[...], preferred_element_type=F32,
                         precision=lax.Precision.HIGHEST) + b_ref[...]


def _adaln(c, w_ada, b_ada):
    nb, d = c.shape
    n = w_ada.shape[1]
    tn = math.gcd(n, 1024)
    return pl.pallas_call(
        _adaln_kernel,
        grid=(n // tn,),
        in_specs=[pl.BlockSpec((nb, d), lambda j: (0, 0)),
                  pl.BlockSpec((d, tn), lambda j: (0, j)),
                  pl.BlockSpec((1, tn), lambda j: (0, j))],
        out_specs=pl.BlockSpec((nb, tn), lambda j: (0, j)),
        out_shape=jax.ShapeDtypeStruct((nb, n), F32),
        compiler_params=_params(("arbitrary",)),
        name="adaln",
    )(c, w_ada, b_ada.reshape(1, n))


def _rms(x, w):
    ms = jnp.mean(x * x, axis=-1, keepdims=True)
    return x * lax.rsqrt(ms + EPS) * w


def _rot(x, c, s):
    return x * c + pltpu.roll(x, 64, 1) * s


def _premix_kernel(x_ref, mod_ref, n1_ref, win_ref, qan_ref, kvan_ref, wuq_ref, qn_ref,
                   wuk_ref, wuv_ref, kn_ref, cm_ref, sm_ref, cr_ref, sr_ref,
                   q_ref, k_ref, v_ref, rq_ref, rk_ref, rv_ref, rg_ref):
    x = x_ref[0]
    shift1 = mod_ref[0, 0:1, :]
    scale1 = mod_ref[0, 1:2, :]
    h = (_rms(x, n1_ref[...]) * (1.0 + scale1) + shift1).astype(BF16)

    o_ckv = Q_LORA
    o_kr = o_ckv + KV_LORA
    o_rq = o_kr + LANE
    hk = RET_HEADS * RET_DK
    hv = RET_HEADS * RET_DV
    o_rk = o_rq + hk
    o_rv = o_rk + hk
    o_rg = o_rv + hv

    def proj(a, b):
        return jnp.dot(h, win_ref[:, a:b], preferred_element_type=F32)

    cm = cm_ref[...]
    sm = sm_ref[...]
    att_scale = MLA_QK ** -0.5

    cq = _rms(proj(0, o_ckv), qan_ref[...]).astype(BF16)
    q_raw = jnp.dot(cq, wuq_ref[...], preferred_element_type=F32)
    qn = qn_ref[...]
    for hd in range(MLA_HEADS):
        qh = q_raw[:, hd * HEAD_PAD:(hd + 1) * HEAD_PAD]
        r = lax.rsqrt(jnp.sum(qh * qh, axis=-1, keepdims=True) * (1.0 / MLA_QK) + EPS)
        qh = qh * (r * att_scale) * qn
        q_ref[0, :, hd * HEAD_PAD:hd * HEAD_PAD + LANE] = qh[:, :LANE].astype(BF16)
        q_ref[0, :, hd * HEAD_PAD + LANE:(hd + 1) * HEAD_PAD] = _rot(qh[:, LANE:], cm, sm).astype(BF16)

    ckv = _rms(proj(o_ckv, o_kr), kvan_ref[...]).astype(BF16)
    k_nope = jnp.dot(ckv, wuk_ref[...], preferred_element_type=F32)
    v_ref[0] = jnp.dot(ckv, wuv_ref[...], preferred_element_type=F32).astype(BF16)
    kr = proj(o_kr, o_rq)
    ss_r = jnp.sum(kr * kr, axis=-1, keepdims=True)
    kn = kn_ref[...]
    kr_rot = _rot(kr * kn[:, LANE:], cm, sm)
    for hd in range(MLA_HEADS):
        kh = k_nope[:, hd * MLA_NOPE:(hd + 1) * MLA_NOPE]
        r = lax.rsqrt((jnp.sum(kh * kh, axis=-1, keepdims=True) + ss_r) * (1.0 / MLA_QK) + EPS)
        k_ref[0, :, hd * HEAD_PAD:hd * HEAD_PAD + LANE] = (kh * r * kn[:, :LANE]).astype(BF16)
        k_ref[0, :, hd * HEAD_PAD + LANE:(hd + 1) * HEAD_PAD] = (kr_rot * r).astype(BF16)

    cr = cr_ref[...]
    sr = sr_ref[...]
    rq = proj(o_rq, o_rk)
    rk = proj(o_rk, o_rv)
    k_scale = RET_DK ** -0.5
    for hd in range(RET_HEADS):
        sl = slice(hd * RET_DK, (hd + 1) * RET_DK)
        rq_ref[0, :, sl] = _rot(rq[:, sl], cr, sr).astype(BF16)
        rk_ref[0, :, sl] = (_rot(rk[:, sl], cr, sr) * k_scale).astype(BF16)
    rv_ref[0] = proj(o_rv, o_rg).astype(BF16)
    rg_ref[0] = proj(o_rg, o_rg + hv).astype(BF16)


def _premix(x, mod, w):
    b, s, d = x.shape
    tm = min(TM_PREMIX, s)
    tok = lambda width: pl.BlockSpec((1, tm, width), lambda i, j: (i, j, 0))
    tab = pl.BlockSpec((tm, LANE), lambda i, j: (j, 0))
    cm, sm, cr, sr = _rope_tables(s)
    hq = MLA_HEADS * HEAD_PAD
    hv = MLA_HEADS * MLA_V
    rk = RET_HEADS * RET_DK
    rv = RET_HEADS * RET_DV
    out_widths = (hq, hq, hv, rk, rk, rv, rv)
    return pl.pallas_call(
        _premix_kernel,
        grid=(b, s // tm),
        in_specs=[tok(d), pl.BlockSpec((1, 6, d), lambda i, j: (i, 0, 0)),
                  _const_spec((1, d)), _const_spec(w["w_in"].shape),
                  _const_spec((1, Q_LORA)), _const_spec((1, KV_LORA)),
                  _const_spec(w["w_uq"].shape), _const_spec((1, HEAD_PAD)),
                  _const_spec(w["w_uk"].shape), _const_spec(w["w_uv"].shape),
                  _const_spec((1, HEAD_PAD)), tab, tab, tab, tab],
        out_specs=[tok(n) for n in out_widths],
        out_shape=[jax.ShapeDtypeStruct((b, s, n), BF16) for n in out_widths],
        compiler_params=_params(("parallel", "parallel")),
        name="premix",
    )(x, mod, w["norm1"], w["w_in"], w["q_a_norm"], w["kv_a_norm"], w["w_uq"], w["q_norm"],
      w["w_uk"], w["w_uv"], w["k_norm"], cm, sm, cr, sr)


def _rope_tables(s):
    pos = jnp.arange(s, dtype=F32)[:, None]

    def cs(dim):
        inv = 1.0 / (ROPE_BASE ** (jnp.arange(0, dim, 2, dtype=F32) / dim))
        ang = pos * inv[None, :]
        return jnp.cos(ang), jnp.sin(ang)

    c, sn = cs(MLA_ROPE)
    z = jnp.zeros_like(c)
    cm = jnp.concatenate([c, z, c, z], axis=-1)
    sm = jnp.concatenate([-sn, z, sn, z], axis=-1)
    c, sn = cs(RET_DK)
    cr = jnp.concatenate([c, c], axis=-1)
    sr = jnp.concatenate([-sn, sn], axis=-1)
    return cm, sm, cr, sr


def _attn_kernel(q_ref, k_ref, v_ref, o_ref):
    s = lax.dot_general(q_ref[0], k_ref[0], (((1,), (1,)), ((), ())), preferred_element_type=F32)
    m = jnp.max(s, axis=-1, keepdims=True)
    p = jnp.exp(s - m)
    l = jnp.sum(p, axis=-1, keepdims=True)
    o = jnp.dot(p.astype(BF16), v_ref[0], preferred_element_type=F32)
    o_ref[0] = (o / l).astype(BF16)


def _attention(q, k, v):
    b, s, _ = q.shape
    tq = min(TQ_ATTN, s)
    return pl.pallas_call(
        _attn_kernel,
        grid=(b, MLA_HEADS, s // tq),
        in_specs=[pl.BlockSpec((1, tq, HEAD_PAD), lambda i, h, j: (i, j, h)),
                  pl.BlockSpec((1, s, HEAD_PAD), lambda i, h, j: (i, 0, h)),
                  pl.BlockSpec((1, s, MLA_V), lambda i, h, j: (i, 0, h))],
        out_specs=pl.BlockSpec((1, tq, MLA_V), lambda i, h, j: (i, j, h)),
        out_shape=jax.ShapeDtypeStruct((b, s, MLA_HEADS * MLA_V), BF16),
        compiler_params=_params(("parallel", "parallel", "arbitrary")),
        name="attention",
    )(q, k, v)


def _retention_kernel(lg_ref, q_ref, k_ref, v_ref, g_ref, gn_ref, o_ref, cb_ref):
    c = RET_CHUNK
    nc = q_ref.shape[1] // c
    logit = lg_ref[0]
    lg = jnp.minimum(logit, 0.0) - jnp.log1p(jnp.exp(-jnp.abs(logit)))
    lg_f = lg[0:1, :]
    lg_b = lg[1:2, :]
    row = lax.broadcasted_iota(jnp.int32, (c, c), 0).astype(F32)
    col = lax.broadcasted_iota(jnp.int32, (c, c), 1).astype(F32)
    rel = row - col
    decay = jnp.where(rel >= 0, jnp.exp(lg_f * jnp.maximum(rel, 0.0)),
                      jnp.exp(lg_b * jnp.maximum(-rel, 0.0)))
    pos = lax.broadcasted_iota(jnp.int32, (c, 1), 0).astype(F32)
    qd_f = jnp.exp(lg_f * (pos + 1.0))
    kd_f = jnp.exp(lg_f * (c - 1.0 - pos))
    qd_b = jnp.exp(lg_b * (c - pos))
    kd_b = jnp.exp(lg_b * pos)
    cd_f = jnp.exp(lg_f * float(c))
    cd_b = jnp.exp(lg_b * float(c))
    gn = gn_ref[...]

    def chunk(ref, n):
        return ref[0, pl.ds(pl.multiple_of(n * c, c), c), :]

    def kv_update(state, kn, vn, kd, cd):
        kt = (kn.astype(F32) * kd).T.astype(BF16)
        return state * cd + jnp.dot(kt, vn, preferred_element_type=F32)

    def bwd_step(i, state):
        n = nc - 1 - i
        qn = chunk(q_ref, n)
        cross = jnp.dot((qn.astype(F32) * qd_b).astype(BF16), state.astype(BF16),
                        preferred_element_type=F32)
        cb_ref[pl.ds(pl.multiple_of(n * c, c), c), :] = cross
        return kv_update(state, chunk(k_ref, n), chunk(v_ref, n), kd_b, cd_b)

    zero = jnp.zeros((RET_DK, RET_DV), F32)
    lax.fori_loop(0, nc, bwd_step, zero)

    def fwd_step(n, state):
        qn = chunk(q_ref, n)
        kn = chunk(k_ref, n)
        vn = chunk(v_ref, n)
        sc = lax.dot_general(qn, kn, (((1,), (1,)), ((), ())), preferred_element_type=F32) * decay
        o = jnp.dot(sc.astype(BF16), vn, preferred_element_type=F32)
        o = o + jnp.dot((qn.astype(F32) * qd_f).astype(BF16), state.astype(BF16),
                        preferred_element_type=F32)
        o = o + cb_ref[pl.ds(pl.multiple_of(n * c, c), c), :]
        mu = jnp.mean(o, axis=-1, keepdims=True)
        d = o - mu
        var = jnp.mean(d * d, axis=-1, keepdims=True)
        on = d * lax.rsqrt(var + EPS) * gn
        g = chunk(g_ref, n).astype(F32)
        o_ref[0, pl.ds(pl.multiple_of(n * c, c), c), :] = (g / (1.0 + jnp.exp(-g)) * on).astype(BF16)
        return kv_update(state, kn, vn, kd_f, cd_f)

    lax.fori_loop(0, nc, fwd_step, zero)


def _retention(rq, rk, rv, rg, w):
    b, s, _ = rq.shape
    hs = lambda width: pl.BlockSpec((1, s, width), lambda i, h: (i, 0, h))
    return pl.pallas_call(
        _retention_kernel,
        grid=(b, RET_HEADS),
        in_specs=[pl.BlockSpec((1, 2, 1), lambda i, h: (h, 0, 0)),
                  hs(RET_DK), hs(RET_DK), hs(RET_DV), hs(RET_DV),
                  pl.BlockSpec((1, RET_DV), lambda i, h: (0, h))],
        out_specs=hs(RET_DV),
        out_shape=jax.ShapeDtypeStruct((b, s, RET_HEADS * RET_DV), BF16),
        scratch_shapes=[pltpu.VMEM((s, RET_DV), F32)],
        compiler_params=_params(("parallel", "parallel")),
        name="retention",
    )(w["ret_logit"], rq, rk, rv, rg, w["ret_gn"])


def _postmix_kernel(x_ref, a_ref, r_ref, mod_ref, wo_ref, n2_ref, x1_ref, h2_ref):
    na = a_ref.shape[2]
    mix = jnp.dot(a_ref[0], wo_ref[:na, :], preferred_element_type=F32)
    mix = mix + jnp.dot(r_ref[0], wo_ref[na:, :], preferred_element_type=F32)
    x1 = x_ref[0] + mod_ref[0, 2:3, :] * mix
    x1_ref[0] = x1
    h2 = _rms(x1, n2_ref[...]) * (1.0 + mod_ref[0, 4:5, :]) + mod_ref[0, 3:4, :]
    h2_ref[0] = h2.astype(BF16)


def _postmix(x, a, r, mod, w):
    b, s, d = x.shape
    tm = min(TM_POSTMIX, s)
    tok = lambda width: pl.BlockSpec((1, tm, width), lambda i, j: (i, j, 0))
    return pl.pallas_call(
        _postmix_kernel,
        grid=(b, s // tm),
        in_specs=[tok(d), tok(a.shape[2]), tok(r.shape[2]),
                  pl.BlockSpec((1, 6, d), lambda i, j: (i, 0, 0)),
                  _const_spec(w["w_o"].shape), _const_spec((1, d))],
        out_specs=[tok(d), tok(d)],
        out_shape=[jax.ShapeDtypeStruct((b, s, d), F32), jax.ShapeDtypeStruct((b, s, d), BF16)],
        compiler_params=_params(("parallel", "parallel")),
        name="postmix",
    )(x, a, r, mod, w["w_o"], w["norm2"])


def _extract_fast(x, vals_ref):
    def body(r, rem):
        m = jnp.max(rem, axis=0, keepdims=True)
        vals_ref[pl.ds(r, 1), :] = m
        return jnp.where(rem == m, -(r.astype(F32) + 1.0) * MARK, rem)

    return lax.fori_loop(0, PEER_TOPK, body, x)


def _extract_exact(x, order, vals_ref):
    def body(r, rem):
        m = jnp.max(rem, axis=0, keepdims=True)
        vals_ref[pl.ds(r, 1), :] = m
        first = jnp.min(jnp.where(rem == m, order, ORDER_END), axis=0, keepdims=True)
        return jnp.where(order == first, -(r.astype(F32) + 1.0) * MARK, rem)

    return lax.fori_loop(0, PEER_TOPK, body, x)


def _marked(rem):
    return (rem <= -0.5 * MARK) & (rem > 0.5 * NOT_A_CANDIDATE)


def _count_marked(rem):
    return jnp.sum(jnp.where(_marked(rem), 1.0, 0.0), axis=0, keepdims=True)


def _bf16_twice(x):
    hi = pltpu.bitcast(x.astype(BF16).astype(F32), jnp.uint32)
    return hi | (hi >> 16)


def _routing_kernel(h2_ref, wq_ref, keys_ref, meta_ref, grp_ref, rta_ref, rtb_ref,
                    pq_ref, a_ref, b_ref, pv_ref, m1_ref, m2_ref, mc_ref):
    k = PEER_TOPK
    hd = pl.program_id(2)
    tm = h2_ref.shape[1]

    @pl.when(hd == 0)
    def _():
        pq = jnp.dot(h2_ref[0], wq_ref[...], preferred_element_type=F32).astype(BF16)
        for i in range(2 * PEER_HEADS):
            pq_ref[i] = pq[:, i * PEER_DKEY:(i + 1) * PEER_DKEY]

    nt = (((1,), (1,)), ((), ()))
    s1 = lax.dot_general(keys_ref[0, 0], pq_ref[2 * hd], nt, preferred_element_type=F32)
    s2 = lax.dot_general(keys_ref[0, 1], pq_ref[2 * hd + 1], nt, preferred_element_type=F32)

    m1_ref[...] = _extract_fast(s1, a_ref)
    m2_ref[...] = _extract_fast(s2, b_ref)
    tied = jnp.max(_count_marked(m1_ref[...]) + _count_marked(m2_ref[...])) > 2.0 * k

    @pl.when(tied)
    def _():
        kiota = lax.broadcasted_iota(jnp.int32, (PEER_NKEYS, tm), 0).astype(F32)
        m1_ref[...] = _extract_exact(s1, kiota, a_ref)
        m2_ref[...] = _extract_exact(s2, kiota, b_ref)

    a16 = a_ref[...]
    b16 = b_ref[...]
    blocks = [a16[0:1, :] + b16[0:8, :], a16[0:1, :] + b16[8:16, :]]
    blocks += [a16[r1:r1 + 1, :] + b16[0:8, :] for r1 in range(1, 8)]
    blocks.append(a16[8:16, :] + b16[0:1, :])
    cand = jnp.where(meta_ref[0] > 0.0, jnp.concatenate(blocks, axis=0), NOT_A_CANDIDATE)
    mc_ref[...] = _extract_fast(cand, pv_ref)
    tied_c = jnp.max(_count_marked(mc_ref[...])) > 1.0 * k

    @pl.when(tied_c)
    def _():
        mc_ref[...] = _extract_exact(cand, meta_ref[1], pv_ref)

    pv = pv_ref[...]
    z = jnp.sum(jnp.exp(pv - pv[0:1, :]), axis=0, keepdims=True)
    picked = jnp.where(_marked(mc_ref[...]), 1.0, 0.0).astype(BF16)
    counts = jnp.dot(grp_ref[...], picked, preferred_element_type=F32).astype(BF16)

    m1 = m1_ref[...]
    top1 = m1 <= -0.5 * MARK
    rank1 = jnp.where(top1, m1 * (-1.0 / MARK) - 1.0, NOT_RANKED).astype(BF16)
    n1 = jnp.zeros((PEER_NKEYS, tm), BF16)
    for r1 in range(k):
        n1 = n1 + jnp.where(rank1 == r1, counts[r1:r1 + 1, :], jnp.zeros((), BF16))
    m2 = m2_ref[...]
    rank2 = jnp.where(m2 <= -0.5 * MARK, m2 * (-1.0 / MARK) - 1.0, NOT_RANKED)
    rta_ref[0, 0, 0] = _bf16_twice(jnp.where(top1, jnp.exp(s1 - a16[0:1, :]), 0.0) / z)
    rta_ref[0, 0, 1] = _bf16_twice(n1)
    rtb_ref[0, 0, 0] = pltpu.bitcast(rank2.astype(BF16), jnp.uint32)
    rtb_ref[0, 0, 1] = pltpu.bitcast(jnp.exp(s2 - b16[0:1, :]).astype(BF16), jnp.uint32)


def _candidate_tables(tm):
    import numpy as np
    k = PEER_TOPK
    pairs = [(0, r2) for r2 in range(16)]
    for r1 in range(1, 8):
        pairs += [(r1, r2) for r2 in range(8)]
    pairs += [(r1, 0) for r1 in range(8, 16)]
    valid = np.array([(r1 + 1) * (r2 + 1) <= k for r1, r2 in pairs], np.float32)
    fidx = np.array([r1 * k + r2 for r1, r2 in pairs], np.float32)
    meta = np.stack([valid, np.where(valid > 0, fidx, ORDER_END)])[:, :, None] * np.ones((1, 1, tm), np.float32)
    grp = np.zeros((k, len(pairs)), np.float32)
    for row, (r1, _) in enumerate(pairs):
        grp[r1, row] = valid[row]
    return jnp.asarray(meta, F32), jnp.asarray(grp, BF16)


def _routing(h2, w):
    b, s, d = h2.shape
    tm = min(TM_ROUTE, s)
    k = PEER_TOPK
    assert k == 16 and PEER_NKEYS >= 16
    meta, grp = _candidate_tables(tm)
    ncand = meta.shape[1]
    out_a = pl.BlockSpec((1, 1, 2, PEER_NKEYS, tm), lambda i, j, h: (i, h, 0, 0, j))
    out_b = pl.BlockSpec((1, 1, 2, PEER_NKEYS // 2, tm), lambda i, j, h: (i, h, 0, 0, j))
    shape_a = (b, PEER_HEADS, 2, PEER_NKEYS, s)
    shape_b = (b, PEER_HEADS, 2, PEER_NKEYS // 2, s)
    return pl.pallas_call(
        _routing_kernel,
        grid=(b, s // tm, PEER_HEADS),
        in_specs=[pl.BlockSpec((1, tm, d), lambda i, j, h: (i, j, 0)),
                  _const_spec(w["peer_wq"].shape),
                  pl.BlockSpec((1, 2, PEER_NKEYS, PEER_DKEY), lambda i, j, h: (h, 0, 0, 0)),
                  _const_spec(meta.shape), _const_spec(grp.shape)],
        out_specs=[out_a, out_b],
        out_shape=[jax.ShapeDtypeStruct(shape_a, jnp.uint32), jax.ShapeDtypeStruct(shape_b, jnp.uint32)],
        scratch_shapes=[pltpu.VMEM((2 * PEER_HEADS, tm, PEER_DKEY), BF16),
                        pltpu.VMEM((k, tm), F32), pltpu.VMEM((k, tm), F32), pltpu.VMEM((k, tm), F32),
                        pltpu.VMEM((PEER_NKEYS, tm), F32), pltpu.VMEM((PEER_NKEYS, tm), F32),
                        pltpu.VMEM((ncand, tm), F32)],
        compiler_params=_params(("parallel", "parallel", "arbitrary")),
        name="peer_routing",
    )(h2, w["peer_wq"], w["peer_keys"], meta, grp)


def _gelu_tanh(x):
    return 0.5 * x * (1.0 + jnp.tanh(math.sqrt(2.0 / math.pi) * (x + 0.044715 * (x * x * x))))


def _peer_kernel(h2_ref, u_ref, vt_ref, rta_ref, rtb_ref, po_ref, acc_ref, y_ref):
    e = pl.program_id(2)
    ne = pl.num_programs(2)
    tm = h2_ref.shape[1]
    d = h2_ref.shape[2]
    keys_per_half = EH_PEER // PEER_NKEYS
    tc = min(TC_PEER, tm)
    words = PEER_NKEYS // 2

    @pl.when(e == 0)
    def _():
        acc_ref[...] = jnp.zeros_like(acc_ref)

    h2 = h2_ref[0]
    for hf in range(EB_PEER // EH_PEER):
        at = lax.dot_general(u_ref[hf * EH_PEER:(hf + 1) * EH_PEER, :], h2, (((1,), (1,)), ((), ())),
                             preferred_element_type=F32)
        for j in range(keys_per_half):
            c = hf * keys_per_half + j
            for lc in range(tm // tc):
                lanes = slice(lc * tc, (lc + 1) * tc)
                wgt = None
                for hd in range(PEER_HEADS):
                    p1 = pltpu.bitcast(jnp.broadcast_to(rta_ref[0, hd, 0, c:c + 1, lanes], (8, tc)), BF16)
                    n1 = pltpu.bitcast(jnp.broadcast_to(rta_ref[0, hd, 1, c:c + 1, lanes], (8, tc)), BF16)
                    r2 = pltpu.bitcast(rtb_ref[0, hd, 0, :, lanes], BF16).reshape(words // 8, BF16_ROWS, tc)
                    p2 = pltpu.bitcast(rtb_ref[0, hd, 1, :, lanes], BF16).reshape(words // 8, BF16_ROWS, tc)
                    w = jnp.where(r2 < n1[None], p1[None] * p2, jnp.zeros((), BF16))
                    wgt = w if wgt is None else wgt + w
                g = _gelu_tanh(at[j * PEER_NKEYS:(j + 1) * PEER_NKEYS, lanes]).astype(BF16)
                y_ref[j * words:(j + 1) * words, lanes] = pltpu.bitcast(wgt.reshape(PEER_NKEYS, tc) * g, jnp.uint32)
        acc_rows = math.gcd(d, ACC_ROWS)
        for rb in range(d // acc_rows):
            rows = slice(rb * acc_rows, (rb + 1) * acc_rows)
            acc_ref[rows, :] += jnp.dot(vt_ref[rows, hf * EH_PEER:(hf + 1) * EH_PEER],
                                        pltpu.bitcast(y_ref[...], BF16), preferred_element_type=F32)

    @pl.when(e == ne - 1)
    def _():
        po_ref[0] = acc_ref[...].T.astype(BF16)


def _peer(h2, rta, rtb, w):
    b, s, d = h2.shape
    tm = min(TM_PEER, s)
    keys_per_step = EB_PEER // PEER_NKEYS
    assert keys_per_step == 8 and EB_PEER % EH_PEER == 0 and EH_PEER % PEER_NKEYS == 0
    ne = w["peer_u"].shape[0] // EB_PEER
    once = pl.Buffered(1)
    tok = lambda **kw: pl.BlockSpec((1, tm, d), lambda i, j, e: (i, j, 0), **kw)
    return pl.pallas_call(
        _peer_kernel,
        grid=(b, s // tm, ne),
        in_specs=[tok(pipeline_mode=once),
                  pl.BlockSpec((EB_PEER, d), lambda i, j, e: (e, 0)),
                  pl.BlockSpec((d, EB_PEER), lambda i, j, e: (0, e)),
                  pl.BlockSpec((1, PEER_HEADS, 2, keys_per_step, tm), lambda i, j, e: (i, 0, 0, e, j)),
                  pl.BlockSpec((1, PEER_HEADS, 2, PEER_NKEYS // 2, tm), lambda i, j, e: (i, 0, 0, 0, j),
                               pipeline_mode=once)],
        out_specs=tok(),
        out_shape=jax.ShapeDtypeStruct((b, s, d), BF16),
        scratch_shapes=[pltpu.VMEM((d, tm), F32), pltpu.VMEM((EH_PEER // 2, tm), jnp.uint32)],
        compiler_params=_params(("parallel", "parallel", "arbitrary")),
        name="peer_experts",
    )(h2, w["peer_u"], w["peer_vt"], rta, rtb)


def _final_kernel(x1_ref, po_ref, mod_ref, y_ref):
    y_ref[0] = x1_ref[0] + mod_ref[0, 5:6, :] * po_ref[0].astype(F32)


def _final(x1, po, mod):
    b, s, d = x1.shape
    tm = min(TM_FINAL, s)
    tok = lambda: pl.BlockSpec((1, tm, d), lambda i, j: (i, j, 0))
    return pl.pallas_call(
        _final_kernel,
        grid=(b, s // tm),
        in_specs=[tok(), tok(), pl.BlockSpec((1, 6, d), lambda i, j: (i, 0, 0))],
        out_specs=tok(),
        out_shape=jax.ShapeDtypeStruct((b, s, d), F32),
        compiler_params=_params(("parallel", "parallel")),
        name="peer_residual",
    )(x1, po, mod)


def _prepare_weights(norm1_w, norm2_w, w_in, q_a_norm, kv_a_norm, w_uq, w_uk, w_uv, q_norm, k_norm,
                     ret_decay_logit, ret_gn_w, w_o, peer_wq, peer_sub_keys, peer_u, peer_v):
    half = MLA_ROPE // 2

    def rope_lanes(t):
        z = jnp.zeros(t.shape[:-1] + (half,), t.dtype)
        return jnp.concatenate([t[..., :half], z, t[..., half:], z], axis=-1)

    def head_lanes(t):
        return jnp.concatenate([t[..., :MLA_NOPE], rope_lanes(t[..., MLA_NOPE:])], axis=-1)

    w_in = w_in[0]
    o = Q_LORA + KV_LORA
    w_in_p = jnp.concatenate([w_in[:, :o], rope_lanes(w_in[:, o:o + MLA_ROPE]), w_in[:, o + MLA_ROPE:]], axis=1)
    w_uq_p = head_lanes(w_uq[0].reshape(Q_LORA, MLA_HEADS, MLA_QK)).reshape(Q_LORA, MLA_HEADS * HEAD_PAD)
    row = lambda t: t.reshape(1, -1).astype(F32)
    return {
        "norm1": row(norm1_w[0]), "norm2": row(norm2_w[0]),
        "w_in": w_in_p.astype(BF16),
        "q_a_norm": row(q_a_norm[0]), "kv_a_norm": row(kv_a_norm[0]),
        "w_uq": w_uq_p.astype(BF16), "w_uk": w_uk[0].astype(BF16), "w_uv": w_uv[0].astype(BF16),
        "q_norm": row(head_lanes(q_norm[0])), "k_norm": row(head_lanes(k_norm[0])),
        "ret_logit": ret_decay_logit[0].T.reshape(RET_HEADS, 2, 1).astype(F32),
        "ret_gn": row(ret_gn_w[0]),
        "w_o": w_o[0].astype(BF16),
        "peer_wq": peer_wq[0].astype(BF16),
        "peer_keys": peer_sub_keys[0].astype(BF16),
        "peer_u": peer_u[0].astype(BF16),
        "peer_vt": peer_v[0].T.astype(BF16),
    }


def _trunk(x, mod, w):
    q, k, v, rq, rk, rv, rg = _premix(x, mod, w)
    a = _attention(q, k, v)
    r = _retention(rq, rk, rv, rg, w)
    x1, h2 = _postmix(x, a, r, mod, w)
    rta, rtb = _routing(h2, w)
    po = _peer(h2, rta, rtb, w)
    return _final(x1, po, mod)


def kernel(x_prompt, x_sample, c_prompt, c_sample, norm1_w, norm2_w, w_ada, b_ada, w_in, q_a_norm, kv_a_norm, w_uq, w_uk, w_uv, q_norm, k_norm, ret_decay_logit, ret_gn_w, w_o, peer_wq, peer_sub_keys, peer_u, peer_v):
    assert w_ada.shape[0] == 1, "single-layer trunk"
    d = x_prompt.shape[-1]
    w = _prepare_weights(norm1_w, norm2_w, w_in, q_a_norm, kv_a_norm, w_uq, w_uk, w_uv, q_norm, k_norm,
                         ret_decay_logit, ret_gn_w, w_o, peer_wq, peer_sub_keys, peer_u, peer_v)
    nbp = c_prompt.shape[0]
    mod = _adaln(jnp.concatenate([c_prompt, c_sample], axis=0), w_ada[0], b_ada[0]).reshape(-1, 6, d)
    y_prompt = _trunk(x_prompt, mod[:nbp], w)
    y_sample = _trunk(x_sample, mod[nbp:], w)
    return (y_prompt, y_sample)
```

```python
import math

import jax
import jax.numpy as jnp
from jax import lax
from jax.experimental import pallas as pl
from jax.experimental.pallas import tpu as pltpu

F32 = jnp.float32
BF16 = jnp.bfloat16

MLA_HEADS = 8
MLA_NOPE = 128
MLA_ROPE = 64
MLA_QK = MLA_NOPE + MLA_ROPE
MLA_V = 128
Q_LORA = 512
KV_LORA = 256
RET_HEADS = 8
RET_DK = 128
RET_DV = 128
RET_CHUNK = 128
PEER_HEADS = 8
PEER_NKEYS = 128
PEER_DKEY = 128
PEER_TOPK = 16
ROPE_BASE = 10000.0
EPS = 1e-6

LANE = 128
HEAD_PAD = 2 * LANE
NOT_RANKED = 99.0
MARK = 2.0 ** 100
NOT_A_CANDIDATE = -(2.0 ** 120)
ORDER_END = 1e9
VMEM_LIMIT = 56 * 1024 * 1024

TM_PREMIX = 256
TQ_ATTN = 512
TQ_SUB = 256
HB_RET = 4
TM_POSTMIX = 256
TM_ROUTE = 256
TM_PEER = 1024
EB_PEER = 1024
EH_PEER = 512
ACC_ROWS = 512
BF16_ROWS = 16
TC_PEER = 256
TM_FINAL = 512


def _params(sem):
    return pltpu.CompilerParams(dimension_semantics=sem, vmem_limit_bytes=VMEM_LIMIT)


def _const_spec(shape):
    nd = len(shape)
    return pl.BlockSpec(shape, lambda *_: (0,) * nd, pipeline_mode=pl.Buffered(1))


def _adaln_kernel(c_ref, w_ref, b_ref, o_ref):
    c = c_ref[...]
    sc = c / (1.0 + jnp.exp(-c))
    o_ref[...] = jnp.dot(sc, w_ref[...], preferred_element_type=F32,
                         precision=lax.Precision.HIGHEST) + b_ref[...]


def _adaln(c, w_ada, b_ada):
    nb, d = c.shape
    n = w_ada.shape[1]
    tn = math.gcd(n, 1024)
    return pl.pallas_call(
        _adaln_kernel,
        grid=(n // tn,),
        in_specs=[pl.BlockSpec((nb, d), lambda j: (0, 0)),
                  pl.BlockSpec((d, tn), lambda j: (0, j)),
                  pl.BlockSpec((1, tn), lambda j: (0, j))],
        out_specs=pl.BlockSpec((nb, tn), lambda j: (0, j)),
        out_shape=jax.ShapeDtypeStruct((nb, n), F32),
        compiler_params=_params(("arbitrary",)),
        name="adaln",
    )(c, w_ada, b_ada.reshape(1, n))


def _rms(x, w):
    ms = jnp.mean(x * x, axis=-1, keepdims=True)
    return x * lax.rsqrt(ms + EPS) * w


def _rot(x, c, s):
    return x * c + pltpu.roll(x, 64, 1) * s


def _premix_kernel(x_ref, mod_ref, n1_ref, win_ref, qan_ref, kvan_ref, wuq_ref, qn_ref,
                   wuk_ref, wuv_ref, kn_ref, cm_ref, sm_ref, cr_ref, sr_ref,
                   q_ref, k_ref, v_ref, rq_ref, rk_ref, rv_ref, rg_ref):
    x = x_ref[0]
    shift1 = mod_ref[0, 0:1, :]
    scale1 = mod_ref[0, 1:2, :]
    h = (_rms(x, n1_ref[...]) * (1.0 + scale1) + shift1).astype(BF16)

    o_ckv = Q_LORA
    o_kr = o_ckv + KV_LORA
    o_rq = o_kr + LANE
    hk = RET_HEADS * RET_DK
    hv = RET_HEADS * RET_DV
    o_rk = o_rq + hk
    o_rv = o_rk + hk
    o_rg = o_rv + hv

    def proj(a, b):
        return jnp.dot(h, win_ref[:, a:b], preferred_element_type=F32)

    cm = cm_ref[...]
    sm = sm_ref[...]
    att_scale = MLA_QK ** -0.5 * math.log2(math.e)

    cq = _rms(proj(0, o_ckv), qan_ref[...]).astype(BF16)
    q_raw = jnp.dot(cq, wuq_ref[...], preferred_element_type=F32)
    qn = qn_ref[...]
    for hd in range(MLA_HEADS):
        qh = q_raw[:, hd * HEAD_PAD:(hd + 1) * HEAD_PAD]
        r = lax.rsqrt(jnp.sum(qh * qh, axis=-1, keepdims=True) * (1.0 / MLA_QK) + EPS)
        qh = qh * (r * att_scale) * qn
        q_ref[0, :, hd * HEAD_PAD:hd * HEAD_PAD + LANE] = qh[:, :LANE].astype(BF16)
        q_ref[0, :, hd * HEAD_PAD + LANE:(hd + 1) * HEAD_PAD] = _rot(qh[:, LANE:], cm, sm).astype(BF16)

    ckv = _rms(proj(o_ckv, o_kr), kvan_ref[...]).astype(BF16)
    k_nope = jnp.dot(ckv, wuk_ref[...], preferred_element_type=F32)
    v_ref[0] = jnp.dot(ckv, wuv_ref[...], preferred_element_type=F32).astype(BF16)
    kr = proj(o_kr, o_rq)
    ss_r = jnp.sum(kr * kr, axis=-1, keepdims=True)
    kn = kn_ref[...]
    kr_rot = _rot(kr * kn[:, LANE:], cm, sm)
    for hd in range(MLA_HEADS):
        kh = k_nope[:, hd * MLA_NOPE:(hd + 1) * MLA_NOPE]
        r = lax.rsqrt((jnp.sum(kh * kh, axis=-1, keepdims=True) + ss_r) * (1.0 / MLA_QK) + EPS)
        k_ref[0, :, hd * HEAD_PAD:hd * HEAD_PAD + LANE] = (kh * r * kn[:, :LANE]).astype(BF16)
        k_ref[0, :, hd * HEAD_PAD + LANE:(hd + 1) * HEAD_PAD] = (kr_rot * r).astype(BF16)

    cr = cr_ref[...]
    sr = sr_ref[...]
    rq = proj(o_rq, o_rk)
    rk = proj(o_rk, o_rv)
    k_scale = RET_DK ** -0.5
    for hd in range(RET_HEADS):
        sl = slice(hd * RET_DK, (hd + 1) * RET_DK)
        rq_ref[0, :, sl] = _rot(rq[:, sl], cr, sr).astype(BF16)
        rk_ref[0, :, sl] = (_rot(rk[:, sl], cr, sr) * k_scale).astype(BF16)
    rv_ref[0] = proj(o_rv, o_rg).astype(BF16)
    rg_ref[0] = proj(o_rg, o_rg + hv).astype(BF16)


def _premix(x, mod, w):
    b, s, d = x.shape
    tm = min(TM_PREMIX, s)
    tok = lambda width: pl.BlockSpec((1, tm, width), lambda i, j: (i, j, 0))
    tab = pl.BlockSpec((tm, LANE), lambda i, j: (j, 0))
    cm, sm, cr, sr = _rope_tables(s)
    hq = MLA_HEADS * HEAD_PAD
    hv = MLA_HEADS * MLA_V
    rk = RET_HEADS * RET_DK
    rv = RET_HEADS * RET_DV
    out_widths = (hq, hq, hv, rk, rk, rv, rv)
    return pl.pallas_call(
        _premix_kernel,
        grid=(b, s // tm),
        in_specs=[tok(d), pl.BlockSpec((1, 6, d), lambda i, j: (i, 0, 0)),
                  _const_spec((1, d)), _const_spec(w["w_in"].shape),
                  _const_spec((1, Q_LORA)), _const_spec((1, KV_LORA)),
                  _const_spec(w["w_uq"].shape), _const_spec((1, HEAD_PAD)),
                  _const_spec(w["w_uk"].shape), _const_spec(w["w_uv"].shape),
                  _const_spec((1, HEAD_PAD)), tab, tab, tab, tab],
        out_specs=[tok(n) for n in out_widths],
        out_shape=[jax.ShapeDtypeStruct((b, s, n), BF16) for n in out_widths],
        compiler_params=_params(("parallel", "parallel")),
        name="premix",
    )(x, mod, w["norm1"], w["w_in"], w["q_a_norm"], w["kv_a_norm"], w["w_uq"], w["q_norm"],
      w["w_uk"], w["w_uv"], w["k_norm"], cm, sm, cr, sr)


def _rope_tables(s):
    pos = jnp.arange(s, dtype=F32)[:, None]

    def cs(dim):
        inv = 1.0 / (ROPE_BASE ** (jnp.arange(0, dim, 2, dtype=F32) / dim))
        ang = pos * inv[None, :]
        return jnp.cos(ang), jnp.sin(ang)

    c, sn = cs(MLA_ROPE)
    z = jnp.zeros_like(c)
    cm = jnp.concatenate([c, z, c, z], axis=-1)
    sm = jnp.concatenate([-sn, z, sn, z], axis=-1)
    c, sn = cs(RET_DK)
    cr = jnp.concatenate([c, c], axis=-1)
    sr = jnp.concatenate([-sn, sn], axis=-1)
    return cm, sm, cr, sr


def _attn_kernel(q_ref, k_ref, v_ref, o_ref):
    k = k_ref[0]
    v = v_ref[0]
    tq = q_ref.shape[1]
    for r in range(tq // TQ_SUB):
        rows = slice(r * TQ_SUB, (r + 1) * TQ_SUB)
        s = lax.dot_general(q_ref[0, rows, :], k, (((1,), (1,)), ((), ())), preferred_element_type=F32)
        m = jnp.max(s, axis=-1, keepdims=True)
        p = jnp.exp2(s - m)
        l = jnp.sum(p, axis=-1, keepdims=True)
        o = jnp.dot(p.astype(BF16), v, preferred_element_type=F32)
        o_ref[0, rows, :] = (o / l).astype(BF16)


def _attention(q, k, v):
    b, s, _ = q.shape
    tq = min(TQ_ATTN, s)
    assert tq % TQ_SUB == 0
    return pl.pallas_call(
        _attn_kernel,
        grid=(b, MLA_HEADS, s // tq),
        in_specs=[pl.BlockSpec((1, tq, HEAD_PAD), lambda i, h, j: (i, j, h)),
                  pl.BlockSpec((1, s, HEAD_PAD), lambda i, h, j: (i, 0, h)),
                  pl.BlockSpec((1, s, MLA_V), lambda i, h, j: (i, 0, h))],
        out_specs=pl.BlockSpec((1, tq, MLA_V), lambda i, h, j: (i, j, h)),
        out_shape=jax.ShapeDtypeStruct((b, s, MLA_HEADS * MLA_V), BF16),
        compiler_params=_params(("parallel", "parallel", "arbitrary")),
        name="attention",
    )(q, k, v)


def _retention_kernel(lg_ref, q_ref, k_ref, v_ref, g_ref, gn_ref, o_ref,
                      of_ref, cb_ref, dec_ref, sf_ref, sb_ref):
    c = RET_CHUNK
    nc = q_ref.shape[1] // c
    hb = q_ref.shape[2] // RET_DK
    row = lax.broadcasted_iota(jnp.int32, (c, c), 0).astype(F32)
    col = lax.broadcasted_iota(jnp.int32, (c, c), 1).astype(F32)
    rel = row - col
    pos = lax.broadcasted_iota(jnp.int32, (c, 1), 0).astype(F32)
    vec = []
    for hh in range(hb):
        logit = lg_ref[hh]
        lg = jnp.minimum(logit, 0.0) - jnp.log1p(jnp.exp(-jnp.abs(logit)))
        lg_f = lg[0:1, :]
        lg_b = lg[1:2, :]
        dec_ref[hh] = jnp.where(rel >= 0, jnp.exp(lg_f * jnp.maximum(rel, 0.0)),
                                jnp.exp(lg_b * jnp.maximum(-rel, 0.0)))
        vec.append(dict(qd_f=jnp.exp(lg_f * (pos + 1.0)), kd_f=jnp.exp(lg_f * (c - 1.0 - pos)),
                        qd_b=jnp.exp(lg_b * (c - pos)), kd_b=jnp.exp(lg_b * pos),
                        cd_f=jnp.exp(lg_f * float(c)), cd_b=jnp.exp(lg_b * float(c))))
    sf_ref[...] = jnp.zeros_like(sf_ref)
    sb_ref[...] = jnp.zeros_like(sb_ref)

    def rows(n):
        return pl.ds(pl.multiple_of(n * c, c), c)

    def kv_update(s_ref, hh, kn, vn, kd, cd):
        kt = (kn.astype(F32) * kd).T.astype(BF16)
        s_ref[hh] = s_ref[hh] * cd + jnp.dot(kt, vn, preferred_element_type=F32)

    def scan(i, carry):
        n = i
        m = nc - 1 - i
        for hh in range(hb):
            lanes = slice(hh * RET_DK, (hh + 1) * RET_DK)
            t = vec[hh]
            qn = q_ref[0, rows(n), lanes]
            kn = k_ref[0, rows(n), lanes]
            vn = v_ref[0, rows(n), lanes]
            sc = lax.dot_general(qn, kn, (((1,), (1,)), ((), ())), preferred_element_type=F32) * dec_ref[hh]
            o = jnp.dot(sc.astype(BF16), vn, preferred_element_type=F32)
            o = o + jnp.dot((qn.astype(F32) * t["qd_f"]).astype(BF16), sf_ref[hh].astype(BF16),
                            preferred_element_type=F32)
            of_ref[rows(n), lanes] = o
            kv_update(sf_ref, hh, kn, vn, t["kd_f"], t["cd_f"])
            qm = q_ref[0, rows(m), lanes]
            cb_ref[rows(m), lanes] = jnp.dot((qm.astype(F32) * t["qd_b"]).astype(BF16), sb_ref[hh].astype(BF16),
                                             preferred_element_type=F32)
            kv_update(sb_ref, hh, k_ref[0, rows(m), lanes], v_ref[0, rows(m), lanes], t["kd_b"], t["cd_b"])
        return carry

    lax.fori_loop(0, nc, scan, 0)

    def finish(n, carry):
        for hh in range(hb):
            lanes = slice(hh * RET_DV, (hh + 1) * RET_DV)
            o = of_ref[rows(n), lanes] + cb_ref[rows(n), lanes]
            mu = jnp.mean(o, axis=-1, keepdims=True)
            dv = o - mu
            var = jnp.mean(dv * dv, axis=-1, keepdims=True)
            on = dv * lax.rsqrt(var + EPS) * gn_ref[:, lanes]
            g = g_ref[0, rows(n), lanes].astype(F32)
            o_ref[0, rows(n), lanes] = (g / (1.0 + jnp.exp(-g)) * on).astype(BF16)
        return carry

    lax.fori_loop(0, nc, finish, 0)


def _retention(rq, rk, rv, rg, w):
    b, s, _ = rq.shape
    hb = min(HB_RET, RET_HEADS)
    assert RET_HEADS % hb == 0 and RET_DK == RET_DV
    hs = lambda **kw: pl.BlockSpec((1, s, hb * RET_DK), lambda i, h: (i, 0, h), **kw)
    once = pl.Buffered(1)
    return pl.pallas_call(
        _retention_kernel,
        grid=(b, RET_HEADS // hb),
        in_specs=[pl.BlockSpec((hb, 2, 1), lambda i, h: (h, 0, 0)),
                  hs(pipeline_mode=once), hs(pipeline_mode=once), hs(pipeline_mode=once), hs(pipeline_mode=once),
                  pl.BlockSpec((1, hb * RET_DV), lambda i, h: (0, h))],
        out_specs=hs(),
        out_shape=jax.ShapeDtypeStruct((b, s, RET_HEADS * RET_DV), BF16),
        scratch_shapes=[pltpu.VMEM((s, hb * RET_DV), F32), pltpu.VMEM((s, hb * RET_DV), F32),
                        pltpu.VMEM((hb, RET_CHUNK, RET_CHUNK), F32),
                        pltpu.VMEM((hb, RET_DK, RET_DV), F32), pltpu.VMEM((hb, RET_DK, RET_DV), F32)],
        compiler_params=_params(("parallel", "parallel")),
        name="retention",
    )(w["ret_logit"], rq, rk, rv, rg, w["ret_gn"])


def _postmix_kernel(x_ref, a_ref, r_ref, mod_ref, wo_ref, n2_ref, x1_ref, h2_ref):
    na = a_ref.shape[2]
    mix = jnp.dot(a_ref[0], wo_ref[:na, :], preferred_element_type=F32)
    mix = mix + jnp.dot(r_ref[0], wo_ref[na:, :], preferred_element_type=F32)
    x1 = x_ref[0] + mod_ref[0, 2:3, :] * mix
    x1_ref[0] = x1
    h2 = _rms(x1, n2_ref[...]) * (1.0 + mod_ref[0, 4:5, :]) + mod_ref[0, 3:4, :]
    h2_ref[0] = h2.astype(BF16)


def _postmix(x, a, r, mod, w):
    b, s, d = x.shape
    tm = min(TM_POSTMIX, s)
    tok = lambda width: pl.BlockSpec((1, tm, width), lambda i, j: (i, j, 0))
    return pl.pallas_call(
        _postmix_kernel,
        grid=(b, s // tm),
        in_specs=[tok(d), tok(a.shape[2]), tok(r.shape[2]),
                  pl.BlockSpec((1, 6, d), lambda i, j: (i, 0, 0)),
                  _const_spec(w["w_o"].shape), _const_spec((1, d))],
        out_specs=[tok(d), tok(d)],
        out_shape=[jax.ShapeDtypeStruct((b, s, d), F32), jax.ShapeDtypeStruct((b, s, d), BF16)],
        compiler_params=_params(("parallel", "parallel")),
        name="postmix",
    )(x, a, r, mod, w["w_o"], w["norm2"])


def _extract_fast(xs, vals_refs):
    def body(r, rems):
        mark = -(r.astype(F32) + 1.0) * MARK
        out = []
        for rem, vals_ref in zip(rems, vals_refs):
            m = jnp.max(rem, axis=0, keepdims=True)
            vals_ref[pl.ds(r, 1), :] = m
            out.append(jnp.where(rem == m, mark, rem))
        return tuple(out)

    return lax.fori_loop(0, PEER_TOPK, body, tuple(xs))


def _extract_exact(x, order, vals_ref):
    def body(r, rem):
        m = jnp.max(rem, axis=0, keepdims=True)
        vals_ref[pl.ds(r, 1), :] = m
        first = jnp.min(jnp.where(rem == m, order, ORDER_END), axis=0, keepdims=True)
        return jnp.where(order == first, -(r.astype(F32) + 1.0) * MARK, rem)

    return lax.fori_loop(0, PEER_TOPK, body, x)


def _marked(rem):
    return (rem <= -0.5 * MARK) & (rem > 0.5 * NOT_A_CANDIDATE)


def _count_marked(rem):
    return jnp.sum(jnp.where(_marked(rem), 1.0, 0.0), axis=0, keepdims=True)


def _bf16_twice(x):
    hi = pltpu.bitcast(x.astype(BF16).astype(F32), jnp.uint32)
    return hi | (hi >> 16)


def _routing_kernel(h2_ref, wq_ref, keys_ref, meta_ref, grp_ref, rta_ref, rtb_ref,
                    pq_ref, a_ref, b_ref, pv_ref, m1_ref, m2_ref, mc_ref):
    k = PEER_TOPK
    hd = pl.program_id(2)
    tm = h2_ref.shape[1]

    @pl.when(hd == 0)
    def _():
        pq = jnp.dot(h2_ref[0], wq_ref[...], preferred_element_type=F32).astype(BF16)
        for i in range(2 * PEER_HEADS):
            pq_ref[i] = pq[:, i * PEER_DKEY:(i + 1) * PEER_DKEY]

    nt = (((1,), (1,)), ((), ()))
    s1 = lax.dot_general(keys_ref[0, 0], pq_ref[2 * hd], nt, preferred_element_type=F32)
    s2 = lax.dot_general(keys_ref[0, 1], pq_ref[2 * hd + 1], nt, preferred_element_type=F32)

    m1_ref[...], m2_ref[...] = _extract_fast((s1, s2), (a_ref, b_ref))
    tied = jnp.max(_count_marked(m1_ref[...]) + _count_marked(m2_ref[...])) > 2.0 * k

    @pl.when(tied)
    def _():
        kiota = lax.broadcasted_iota(jnp.int32, (PEER_NKEYS, tm), 0).astype(F32)
        m1_ref[...] = _extract_exact(s1, kiota, a_ref)
        m2_ref[...] = _extract_exact(s2, kiota, b_ref)

    a16 = a_ref[...]
    b16 = b_ref[...]
    blocks = [a16[0:1, :] + b16[0:8, :], a16[0:1, :] + b16[8:16, :]]
    blocks += [a16[r1:r1 + 1, :] + b16[0:8, :] for r1 in range(1, 8)]
    blocks.append(a16[8:16, :] + b16[0:1, :])
    cand = jnp.where(meta_ref[0] > 0.0, jnp.concatenate(blocks, axis=0), NOT_A_CANDIDATE)
    mc_ref[...], = _extract_fast((cand,), (pv_ref,))
    tied_c = jnp.max(_count_marked(mc_ref[...])) > 1.0 * k

    @pl.when(tied_c)
    def _():
        mc_ref[...] = _extract_exact(cand, meta_ref[1], pv_ref)

    pv = pv_ref[...]
    z = jnp.sum(jnp.exp(pv - pv[0:1, :]), axis=0, keepdims=True)
    picked = jnp.where(_marked(mc_ref[...]), 1.0, 0.0).astype(BF16)
    counts = jnp.dot(grp_ref[...], picked, preferred_element_type=F32).astype(BF16)

    m1 = m1_ref[...]
    top1 = m1 <= -0.5 * MARK
    rank1 = jnp.where(top1, m1 * (-1.0 / MARK) - 1.0, NOT_RANKED).astype(BF16)
    n1 = jnp.zeros((PEER_NKEYS, tm), BF16)
    for r1 in range(k):
        n1 = n1 + jnp.where(rank1 == r1, counts[r1:r1 + 1, :], jnp.zeros((), BF16))
    m2 = m2_ref[...]
    rank2 = jnp.where(m2 <= -0.5 * MARK, m2 * (-1.0 / MARK) - 1.0, NOT_RANKED)
    rta_ref[0, 0, 0] = _bf16_twice(jnp.where(top1, jnp.exp(s1 - a16[0:1, :]), 0.0) / z)
    rta_ref[0, 0, 1] = _bf16_twice(n1)
    rtb_ref[0, 0, 0] = pltpu.bitcast(rank2.astype(BF16), jnp.uint32)
    rtb_ref[0, 0, 1] = pltpu.bitcast(jnp.exp(s2 - b16[0:1, :]).astype(BF16), jnp.uint32)


def _candidate_tables(tm):
    import numpy as np
    k = PEER_TOPK
    pairs = [(0, r2) for r2 in range(16)]
    for r1 in range(1, 8):
        pairs += [(r1, r2) for r2 in range(8)]
    pairs += [(r1, 0) for r1 in range(8, 16)]
    valid = np.array([(r1 + 1) * (r2 + 1) <= k for r1, r2 in pairs], np.float32)
    fidx = np.array([r1 * k + r2 for r1, r2 in pairs], np.float32)
    meta = np.stack([valid, np.where(valid > 0, fidx, ORDER_END)])[:, :, None] * np.ones((1, 1, tm), np.float32)
    grp = np.zeros((k, len(pairs)), np.float32)
    for row, (r1, _) in enumerate(pairs):
        grp[r1, row] = valid[row]
    return jnp.asarray(meta, F32), jnp.asarray(grp, BF16)


def _routing(h2, w):
    b, s, d = h2.shape
    tm = min(TM_ROUTE, s)
    k = PEER_TOPK
    assert k == 16 and PEER_NKEYS >= 16
    meta, grp = _candidate_tables(tm)
    ncand = meta.shape[1]
    out_a = pl.BlockSpec((1, 1, 2, PEER_NKEYS, tm), lambda i, j, h: (i, h, 0, 0, j))
    out_b = pl.BlockSpec((1, 1, 2, PEER_NKEYS // 2, tm), lambda i, j, h: (i, h, 0, 0, j))
    shape_a = (b, PEER_HEADS, 2, PEER_NKEYS, s)
    shape_b = (b, PEER_HEADS, 2, PEER_NKEYS // 2, s)
    return pl.pallas_call(
        _routing_kernel,
        grid=(b, s // tm, PEER_HEADS),
        in_specs=[pl.BlockSpec((1, tm, d), lambda i, j, h: (i, j, 0)),
                  _const_spec(w["peer_wq"].shape),
                  pl.BlockSpec((1, 2, PEER_NKEYS, PEER_DKEY), lambda i, j, h: (h, 0, 0, 0)),
                  _const_spec(meta.shape), _const_spec(grp.shape)],
        out_specs=[out_a, out_b],
        out_shape=[jax.ShapeDtypeStruct(shape_a, jnp.uint32), jax.ShapeDtypeStruct(shape_b, jnp.uint32)],
        scratch_shapes=[pltpu.VMEM((2 * PEER_HEADS, tm, PEER_DKEY), BF16),
                        pltpu.VMEM((k, tm), F32), pltpu.VMEM((k, tm), F32), pltpu.VMEM((k, tm), F32),
                        pltpu.VMEM((PEER_NKEYS, tm), F32), pltpu.VMEM((PEER_NKEYS, tm), F32),
                        pltpu.VMEM((ncand, tm), F32)],
        compiler_params=_params(("parallel", "parallel", "arbitrary")),
        name="peer_routing",
    )(h2, w["peer_wq"], w["peer_keys"], meta, grp)


def _gelu_tanh(x):
    return 0.5 * x * (1.0 + jnp.tanh(math.sqrt(2.0 / math.pi) * (x + 0.044715 * (x * x * x))))


def _peer_kernel(h2_ref, u_ref, vt_ref, rta_ref, rtb_ref, po_ref, acc_ref, y_ref):
    e = pl.program_id(2)
    ne = pl.num_programs(2)
    tm = h2_ref.shape[1]
    d = h2_ref.shape[2]
    keys_per_half = EH_PEER // PEER_NKEYS
    tc = min(TC_PEER, tm)
    words = PEER_NKEYS // 2

    @pl.when(e == 0)
    def _():
        acc_ref[...] = jnp.zeros_like(acc_ref)

    h2 = h2_ref[0]
    for hf in range(EB_PEER // EH_PEER):
        at = lax.dot_general(u_ref[hf * EH_PEER:(hf + 1) * EH_PEER, :], h2, (((1,), (1,)), ((), ())),
                             preferred_element_type=F32)
        for j in range(keys_per_half):
            c = hf * keys_per_half + j
            for lc in range(tm // tc):
                lanes = slice(lc * tc, (lc + 1) * tc)
                wgt = None
                for hd in range(PEER_HEADS):
                    p1 = pltpu.bitcast(jnp.broadcast_to(rta_ref[0, hd, 0, c:c + 1, lanes], (8, tc)), BF16)
                    n1 = pltpu.bitcast(jnp.broadcast_to(rta_ref[0, hd, 1, c:c + 1, lanes], (8, tc)), BF16)
                    r2 = pltpu.bitcast(rtb_ref[0, hd, 0, :, lanes], BF16).reshape(words // 8, BF16_ROWS, tc)
                    p2 = pltpu.bitcast(rtb_ref[0, hd, 1, :, lanes], BF16).reshape(words // 8, BF16_ROWS, tc)
                    w = jnp.where(r2 < n1[None], p1[None] * p2, jnp.zeros((), BF16))
                    wgt = w if wgt is None else wgt + w
                g = _gelu_tanh(at[j * PEER_NKEYS:(j + 1) * PEER_NKEYS, lanes]).astype(BF16)
                y_ref[j * words:(j + 1) * words, lanes] = pltpu.bitcast(wgt.reshape(PEER_NKEYS, tc) * g, jnp.uint32)
        acc_rows = math.gcd(d, ACC_ROWS)
        for rb in range(d // acc_rows):
            rows = slice(rb * acc_rows, (rb + 1) * acc_rows)
            acc_ref[rows, :] += jnp.dot(vt_ref[rows, hf * EH_PEER:(hf + 1) * EH_PEER],
                                        pltpu.bitcast(y_ref[...], BF16), preferred_element_type=F32)

    @pl.when(e == ne - 1)
    def _():
        po_ref[0] = acc_ref[...].T.astype(BF16)


def _peer(h2, rta, rtb, w):
    b, s, d = h2.shape
    tm = min(TM_PEER, s)
    keys_per_step = EB_PEER // PEER_NKEYS
    assert keys_per_step == 8 and EB_PEER % EH_PEER == 0 and EH_PEER % PEER_NKEYS == 0
    ne = w["peer_u"].shape[0] // EB_PEER
    once = pl.Buffered(1)
    tok = lambda **kw: pl.BlockSpec((1, tm, d), lambda i, j, e: (i, j, 0), **kw)
    return pl.pallas_call(
        _peer_kernel,
        grid=(b, s // tm, ne),
        in_specs=[tok(pipeline_mode=once),
                  pl.BlockSpec((EB_PEER, d), lambda i, j, e: (e, 0)),
                  pl.BlockSpec((d, EB_PEER), lambda i, j, e: (0, e)),
                  pl.BlockSpec((1, PEER_HEADS, 2, keys_per_step, tm), lambda i, j, e: (i, 0, 0, e, j)),
                  pl.BlockSpec((1, PEER_HEADS, 2, PEER_NKEYS // 2, tm), lambda i, j, e: (i, 0, 0, 0, j),
                               pipeline_mode=once)],
        out_specs=tok(),
        out_shape=jax.ShapeDtypeStruct((b, s, d), BF16),
        scratch_shapes=[pltpu.VMEM((d, tm), F32), pltpu.VMEM((EH_PEER // 2, tm), jnp.uint32)],
        compiler_params=_params(("parallel", "parallel", "arbitrary")),
        name="peer_experts",
    )(h2, w["peer_u"], w["peer_vt"], rta, rtb)


def _final_kernel(x1_ref, po_ref, mod_ref, y_ref):
    y_ref[0] = x1_ref[0] + mod_ref[0, 5:6, :] * po_ref[0].astype(F32)


def _final(x1, po, mod):
    b, s, d = x1.shape
    tm = min(TM_FINAL, s)
    tok = lambda: pl.BlockSpec((1, tm, d), lambda i, j: (i, j, 0))
    return pl.pallas_call(
        _final_kernel,
        grid=(b, s // tm),
        in_specs=[tok(), tok(), pl.BlockSpec((1, 6, d), lambda i, j: (i, 0, 0))],
        out_specs=tok(),
        out_shape=jax.ShapeDtypeStruct((b, s, d), F32),
        compiler_params=_params(("parallel", "parallel")),
        name="peer_residual",
    )(x1, po, mod)


def _prepare_weights(norm1_w, norm2_w, w_in, q_a_norm, kv_a_norm, w_uq, w_uk, w_uv, q_norm, k_norm,
                     ret_decay_logit, ret_gn_w, w_o, peer_wq, peer_sub_keys, peer_u, peer_v):
    half = MLA_ROPE // 2

    def rope_lanes(t):
        z = jnp.zeros(t.shape[:-1] + (half,), t.dtype)
        return jnp.concatenate([t[..., :half], z, t[..., half:], z], axis=-1)

    def head_lanes(t):
        return jnp.concatenate([t[..., :MLA_NOPE], rope_lanes(t[..., MLA_NOPE:])], axis=-1)

    w_in = w_in[0]
    o = Q_LORA + KV_LORA
    w_in_p = jnp.concatenate([w_in[:, :o], rope_lanes(w_in[:, o:o + MLA_ROPE]), w_in[:, o + MLA_ROPE:]], axis=1)
    w_uq_p = head_lanes(w_uq[0].reshape(Q_LORA, MLA_HEADS, MLA_QK)).reshape(Q_LORA, MLA_HEADS * HEAD_PAD)
    row = lambda t: t.reshape(1, -1).astype(F32)
    return {
        "norm1": row(norm1_w[0]), "norm2": row(norm2_w[0]),
        "w_in": w_in_p.astype(BF16),
        "q_a_norm": row(q_a_norm[0]), "kv_a_norm": row(kv_a_norm[0]),
        "w_uq": w_uq_p.astype(BF16), "w_uk": w_uk[0].astype(BF16), "w_uv": w_uv[0].astype(BF16),
        "q_norm": row(head_lanes(q_norm[0])), "k_norm": row(head_lanes(k_norm[0])),
        "ret_logit": ret_decay_logit[0].T.reshape(RET_HEADS, 2, 1).astype(F32),
        "ret_gn": row(ret_gn_w[0]),
        "w_o": w_o[0].astype(BF16),
        "peer_wq": peer_wq[0].astype(BF16),
        "peer_keys": peer_sub_keys[0].astype(BF16),
        "peer_u": peer_u[0].astype(BF16),
        "peer_vt": peer_v[0].T.astype(BF16),
    }


def _trunk(x, mod, w):
    q, k, v, rq, rk, rv, rg = _premix(x, mod, w)
    a = _attention(q, k, v)
    r = _retention(rq, rk, rv, rg, w)
    x1, h2 = _postmix(x, a, r, mod, w)
    rta, rtb = _routing(h2, w)
    po = _peer(h2, rta, rtb, w)
    return _final(x1, po, mod)


def kernel(x_prompt, x_sample, c_prompt, c_sample, norm1_w, norm2_w, w_ada, b_ada, w_in, q_a_norm, kv_a_norm, w_uq, w_uk, w_uv, q_norm, k_norm, ret_decay_logit, ret_gn_w, w_o, peer_wq, peer_sub_keys, peer_u, peer_v):
    assert w_ada.shape[0] == 1, "single-layer trunk"
    d = x_prompt.shape[-1]
    w = _prepare_weights(norm1_w, norm2_w, w_in, q_a_norm, kv_a_norm, w_uq, w_uk, w_uv, q_norm, k_norm,
                         ret_decay_logit, ret_gn_w, w_o, peer_wq, peer_sub_keys, peer_u, peer_v)
    nbp = c_prompt.shape[0]
    mod = _adaln(jnp.concatenate([c_prompt, c_sample], axis=0), w_ada[0], b_ada[0]).reshape(-1, 6, d)
    y_prompt = _trunk(x_prompt, mod[:nbp], w)
    y_sample = _trunk(x_sample, mod[nbp:], w)
    return (y_prompt, y_sample)
```

```python
import math

import jax
import jax.numpy as jnp
from jax import lax
from jax.experimental import pallas as pl
from jax.experimental.pallas import tpu as pltpu

F32 = jnp.float32
BF16 = jnp.bfloat16

MLA_HEADS = 8
MLA_NOPE = 128
MLA_ROPE = 64
MLA_QK = MLA_NOPE + MLA_ROPE
MLA_V = 128
Q_LORA = 512
KV_LORA = 256
RET_HEADS = 8
RET_DK = 128
RET_DV = 128
RET_CHUNK = 128
PEER_HEADS = 8
PEER_NKEYS = 128
PEER_DKEY = 128
PEER_TOPK = 16
ROPE_BASE = 10000.0
EPS = 1e-6

LANE = 128
HEAD_PAD = 2 * LANE
NOT_RANKED = 99.0
MARK = 2.0 ** 100
NOT_A_CANDIDATE = -(2.0 ** 120)
ORDER_END = 1e9
VMEM_LIMIT = 56 * 1024 * 1024

TM_PREMIX = 256
TQ_ATTN = 512
TQ_SUB = 256
HB_RET = 4
TM_POSTMIX = 256
TM_ROUTE = 256
TM_PEER = 1024
EB_PEER = 1024
EH_PEER = 512
ACC_ROWS = 512
BF16_ROWS = 16
TC_PEER = 256
TM_FINAL = 512


def _params(sem):
    return pltpu.CompilerParams(dimension_semantics=sem, vmem_limit_bytes=VMEM_LIMIT)


def _const_spec(shape):
    nd = len(shape)
    return pl.BlockSpec(shape, lambda *_: (0,) * nd, pipeline_mode=pl.Buffered(1))


def _adaln_kernel(c_ref, w_ref, b_ref, o_ref):
    c = c_ref[...]
    sc = c / (1.0 + jnp.exp(-c))
    o_ref[...] = jnp.dot(sc, w_ref[...], preferred_element_type=F32,
                         precision=lax.Precision.HIGHEST) + b_ref[...]


def _adaln(c, w_ada, b_ada):
    nb, d = c.shape
    n = w_ada.shape[1]
    tn = math.gcd(n, 1024)
    return pl.pallas_call(
        _adaln_kernel,
        grid=(n // tn,),
        in_specs=[pl.BlockSpec((nb, d), lambda j: (0, 0)),
                  pl.BlockSpec((d, tn), lambda j: (0, j)),
                  pl.BlockSpec((1, tn), lambda j: (0, j))],
        out_specs=pl.BlockSpec((nb, tn), lambda j: (0, j)),
        out_shape=jax.ShapeDtypeStruct((nb, n), F32),
        compiler_params=_params(("arbitrary",)),
        name="adaln",
    )(c, w_ada, b_ada.reshape(1, n))


def _rms(x, w):
    ms = jnp.mean(x * x, axis=-1, keepdims=True)
    return x * lax.rsqrt(ms + EPS) * w


def _rot(x, c, s):
    return x * c + pltpu.roll(x, 64, 1) * s


def _premix_kernel(x_ref, mod_ref, n1_ref, win_ref, qan_ref, kvan_ref, wuq_ref, qn_ref,
                   wuk_ref, wuv_ref, kn_ref, cm_ref, sm_ref, cr_ref, sr_ref,
                   q_ref, k_ref, v_ref, rq_ref, rk_ref, rv_ref, rg_ref):
    x = x_ref[0]
    shift1 = mod_ref[0, 0:1, :]
    scale1 = mod_ref[0, 1:2, :]
    h = (_rms(x, n1_ref[...]) * (1.0 + scale1) + shift1).astype(BF16)

    o_ckv = Q_LORA
    o_kr = o_ckv + KV_LORA
    o_rq = o_kr + LANE
    hk = RET_HEADS * RET_DK
    hv = RET_HEADS * RET_DV
    o_rk = o_rq + hk
    o_rv = o_rk + hk
    o_rg = o_rv + hv

    def proj(a, b):
        return jnp.dot(h, win_ref[:, a:b], preferred_element_type=F32)

    cm = cm_ref[...]
    sm = sm_ref[...]
    att_scale = MLA_QK ** -0.5 * math.log2(math.e)

    cq = _rms(proj(0, o_ckv), qan_ref[...]).astype(BF16)
    q_raw = jnp.dot(cq, wuq_ref[...], preferred_element_type=F32)
    qn = qn_ref[...]
    for hd in range(MLA_HEADS):
        qh = q_raw[:, hd * HEAD_PAD:(hd + 1) * HEAD_PAD]
        r = lax.rsqrt(jnp.sum(qh * qh, axis=-1, keepdims=True) * (1.0 / MLA_QK) + EPS)
        qh = qh * (r * att_scale) * qn
        q_ref[0, :, hd * HEAD_PAD:hd * HEAD_PAD + LANE] = qh[:, :LANE].astype(BF16)
        q_ref[0, :, hd * HEAD_PAD + LANE:(hd + 1) * HEAD_PAD] = _rot(qh[:, LANE:], cm, sm).astype(BF16)

    ckv = _rms(proj(o_ckv, o_kr), kvan_ref[...]).astype(BF16)
    k_nope = jnp.dot(ckv, wuk_ref[...], preferred_element_type=F32)
    v_ref[0] = jnp.dot(ckv, wuv_ref[...], preferred_element_type=F32).astype(BF16)
    kr = proj(o_kr, o_rq)
    ss_r = jnp.sum(kr * kr, axis=-1, keepdims=True)
    kn = kn_ref[...]
    kr_rot = _rot(kr * kn[:, LANE:], cm, sm)
    for hd in range(MLA_HEADS):
        kh = k_nope[:, hd * MLA_NOPE:(hd + 1) * MLA_NOPE]
        r = lax.rsqrt((jnp.sum(kh * kh, axis=-1, keepdims=True) + ss_r) * (1.0 / MLA_QK) + EPS)
        k_ref[0, :, hd * HEAD_PAD:hd * HEAD_PAD + LANE] = (kh * r * kn[:, :LANE]).astype(BF16)
        k_ref[0, :, hd * HEAD_PAD + LANE:(hd + 1) * HEAD_PAD] = (kr_rot * r).astype(BF16)

    cr = cr_ref[...]
    sr = sr_ref[...]
    rq = proj(o_rq, o_rk)
    rk = proj(o_rk, o_rv)
    k_scale = RET_DK ** -0.5
    for hd in range(RET_HEADS):
        sl = slice(hd * RET_DK, (hd + 1) * RET_DK)
        rq_ref[0, :, sl] = _rot(rq[:, sl], cr, sr).astype(BF16)
        rk_ref[0, :, sl] = (_rot(rk[:, sl], cr, sr) * k_scale).astype(BF16)
    rv_ref[0] = proj(o_rv, o_rg).astype(BF16)
    rg_ref[0] = proj(o_rg, o_rg + hv).astype(BF16)


def _premix(x, mod, w):
    b, s, d = x.shape
    tm = min(TM_PREMIX, s)
    tok = lambda width: pl.BlockSpec((1, tm, width), lambda i, j: (i, j, 0))
    tab = pl.BlockSpec((tm, LANE), lambda i, j: (j, 0))
    cm, sm, cr, sr = _rope_tables(s)
    hq = MLA_HEADS * HEAD_PAD
    hv = MLA_HEADS * MLA_V
    rk = RET_HEADS * RET_DK
    rv = RET_HEADS * RET_DV
    out_widths = (hq, hq, hv, rk, rk, rv, rv)
    return pl.pallas_call(
        _premix_kernel,
        grid=(b, s // tm),
        in_specs=[tok(d), pl.BlockSpec((1, 6, d), lambda i, j: (i, 0, 0)),
                  _const_spec((1, d)), _const_spec(w["w_in"].shape),
                  _const_spec((1, Q_LORA)), _const_spec((1, KV_LORA)),
                  _const_spec(w["w_uq"].shape), _const_spec((1, HEAD_PAD)),
                  _const_spec(w["w_uk"].shape), _const_spec(w["w_uv"].shape),
                  _const_spec((1, HEAD_PAD)), tab, tab, tab, tab],
        out_specs=[tok(n) for n in out_widths],
        out_shape=[jax.ShapeDtypeStruct((b, s, n), BF16) for n in out_widths],
        compiler_params=_params(("parallel", "parallel")),
        name="premix",
    )(x, mod, w["norm1"], w["w_in"], w["q_a_norm"], w["kv_a_norm"], w["w_uq"], w["q_norm"],
      w["w_uk"], w["w_uv"], w["k_norm"], cm, sm, cr, sr)


def _rope_tables(s):
    pos = jnp.arange(s, dtype=F32)[:, None]

    def cs(dim):
        inv = 1.0 / (ROPE_BASE ** (jnp.arange(0, dim, 2, dtype=F32) / dim))
        ang = pos * inv[None, :]
        return jnp.cos(ang), jnp.sin(ang)

    c, sn = cs(MLA_ROPE)
    z = jnp.zeros_like(c)
    cm = jnp.concatenate([c, z, c, z], axis=-1)
    sm = jnp.concatenate([-sn, z, sn, z], axis=-1)
    c, sn = cs(RET_DK)
    cr = jnp.concatenate([c, c], axis=-1)
    sr = jnp.concatenate([-sn, sn], axis=-1)
    return cm, sm, cr, sr


def _attn_kernel(q_ref, k_ref, v_ref, o_ref):
    k = k_ref[0]
    v = v_ref[0]
    tq = q_ref.shape[1]
    for r in range(tq // TQ_SUB):
        rows = slice(r * TQ_SUB, (r + 1) * TQ_SUB)
        s = lax.dot_general(q_ref[0, rows, :], k, (((1,), (1,)), ((), ())), preferred_element_type=F32)
        m = jnp.max(s, axis=-1, keepdims=True)
        p = jnp.exp2(s - m)
        l = jnp.sum(p, axis=-1, keepdims=True)
        o = jnp.dot(p.astype(BF16), v, preferred_element_type=F32)
        o_ref[0, rows, :] = (o / l).astype(BF16)


def _attention(q, k, v):
    b, s, _ = q.shape
    tq = min(TQ_ATTN, s)
    assert tq % TQ_SUB == 0
    return pl.pallas_call(
        _attn_kernel,
        grid=(b, MLA_HEADS, s // tq),
        in_specs=[pl.BlockSpec((1, tq, HEAD_PAD), lambda i, h, j: (i, j, h)),
                  pl.BlockSpec((1, s, HEAD_PAD), lambda i, h, j: (i, 0, h)),
                  pl.BlockSpec((1, s, MLA_V), lambda i, h, j: (i, 0, h))],
        out_specs=pl.BlockSpec((1, tq, MLA_V), lambda i, h, j: (i, j, h)),
        out_shape=jax.ShapeDtypeStruct((b, s, MLA_HEADS * MLA_V), BF16),
        compiler_params=_params(("parallel", "parallel", "arbitrary")),
        name="attention",
    )(q, k, v)


def _retention_kernel(lg_ref, q_ref, k_ref, v_ref, g_ref, gn_ref, o_ref,
                      of_ref, cb_ref, dec_ref, sf_ref, sb_ref):
    c = RET_CHUNK
    nc = q_ref.shape[1] // c
    hb = q_ref.shape[2] // RET_DK
    row = lax.broadcasted_iota(jnp.int32, (c, c), 0).astype(F32)
    col = lax.broadcasted_iota(jnp.int32, (c, c), 1).astype(F32)
    rel = row - col
    pos = lax.broadcasted_iota(jnp.int32, (c, 1), 0).astype(F32)
    vec = []
    for hh in range(hb):
        logit = lg_ref[hh]
        lg = jnp.minimum(logit, 0.0) - jnp.log1p(jnp.exp(-jnp.abs(logit)))
        lg_f = lg[0:1, :]
        lg_b = lg[1:2, :]
        dec_ref[hh] = jnp.where(rel >= 0, jnp.exp(lg_f * jnp.maximum(rel, 0.0)),
                                jnp.exp(lg_b * jnp.maximum(-rel, 0.0)))
        vec.append(dict(qd_f=jnp.exp(lg_f * (pos + 1.0)), kd_f=jnp.exp(lg_f * (c - 1.0 - pos)),
                        qd_b=jnp.exp(lg_b * (c - pos)), kd_b=jnp.exp(lg_b * pos),
                        cd_f=jnp.exp(lg_f * float(c)), cd_b=jnp.exp(lg_b * float(c))))
    sf_ref[...] = jnp.zeros_like(sf_ref)
    sb_ref[...] = jnp.zeros_like(sb_ref)

    def rows(n):
        return pl.ds(pl.multiple_of(n * c, c), c)

    def kv_update(s_ref, hh, kn, vn, kd, cd):
        kt = (kn.astype(F32) * kd).T.astype(BF16)
        s_ref[hh] = s_ref[hh] * cd + jnp.dot(kt, vn, preferred_element_type=F32)

    def scan(i, carry):
        n = i
        m = nc - 1 - i
        for hh in range(hb):
            lanes = slice(hh * RET_DK, (hh + 1) * RET_DK)
            t = vec[hh]
            qn = q_ref[0, rows(n), lanes]
            kn = k_ref[0, rows(n), lanes]
            vn = v_ref[0, rows(n), lanes]
            sc = lax.dot_general(qn, kn, (((1,), (1,)), ((), ())), preferred_element_type=F32) * dec_ref[hh]
            o = jnp.dot(sc.astype(BF16), vn, preferred_element_type=F32)
            o = o + jnp.dot((qn.astype(F32) * t["qd_f"]).astype(BF16), sf_ref[hh].astype(BF16),
                            preferred_element_type=F32)
            of_ref[rows(n), lanes] = o
            kv_update(sf_ref, hh, kn, vn, t["kd_f"], t["cd_f"])
            qm = q_ref[0, rows(m), lanes]
            cb_ref[rows(m), lanes] = jnp.dot((qm.astype(F32) * t["qd_b"]).astype(BF16), sb_ref[hh].astype(BF16),
                                             preferred_element_type=F32)
            kv_update(sb_ref, hh, k_ref[0, rows(m), lanes], v_ref[0, rows(m), lanes], t["kd_b"], t["cd_b"])
        return carry

    lax.fori_loop(0, nc, scan, 0)

    def finish(n, carry):
        for hh in range(hb):
            lanes = slice(hh * RET_DV, (hh + 1) * RET_DV)
            o = of_ref[rows(n), lanes] + cb_ref[rows(n), lanes]
            mu = jnp.mean(o, axis=-1, keepdims=True)
            dv = o - mu
            var = jnp.mean(dv * dv, axis=-1, keepdims=True)
            on = dv * lax.rsqrt(var + EPS) * gn_ref[:, lanes]
            g = g_ref[0, rows(n), lanes].astype(F32)
            o_ref[0, rows(n), lanes] = (g / (1.0 + jnp.exp(-g)) * on).astype(BF16)
        return carry

    lax.fori_loop(0, nc, finish, 0)


def _retention(rq, rk, rv, rg, w):
    b, s, _ = rq.shape
    hb = min(HB_RET, RET_HEADS)
    assert RET_HEADS % hb == 0 and RET_DK == RET_DV
    hs = lambda **kw: pl.BlockSpec((1, s, hb * RET_DK), lambda i, h: (i, 0, h), **kw)
    once = pl.Buffered(1)
    return pl.pallas_call(
        _retention_kernel,
        grid=(b, RET_HEADS // hb),
        in_specs=[pl.BlockSpec((hb, 2, 1), lambda i, h: (h, 0, 0)),
                  hs(pipeline_mode=once), hs(pipeline_mode=once), hs(pipeline_mode=once), hs(pipeline_mode=once),
                  pl.BlockSpec((1, hb * RET_DV), lambda i, h: (0, h))],
        out_specs=hs(),
        out_shape=jax.ShapeDtypeStruct((b, s, RET_HEADS * RET_DV), BF16),
        scratch_shapes=[pltpu.VMEM((s, hb * RET_DV), F32), pltpu.VMEM((s, hb * RET_DV), F32),
                        pltpu.VMEM((hb, RET_CHUNK, RET_CHUNK), F32),
                        pltpu.VMEM((hb, RET_DK, RET_DV), F32), pltpu.VMEM((hb, RET_DK, RET_DV), F32)],
        compiler_params=_params(("parallel", "parallel")),
        name="retention",
    )(w["ret_logit"], rq, rk, rv, rg, w["ret_gn"])


def _postmix_kernel(x_ref, a_ref, r_ref, mod_ref, wo_ref, n2_ref, x1_ref, h2_ref):
    na = a_ref.shape[2]
    mix = jnp.dot(a_ref[0], wo_ref[:na, :], preferred_element_type=F32)
    mix = mix + jnp.dot(r_ref[0], wo_ref[na:, :], preferred_element_type=F32)
    x1 = x_ref[0] + mod_ref[0, 2:3, :] * mix
    x1_ref[0] = x1
    h2 = _rms(x1, n2_ref[...]) * (1.0 + mod_ref[0, 4:5, :]) + mod_ref[0, 3:4, :]
    h2_ref[0] = h2.astype(BF16)


def _postmix(x, a, r, mod, w):
    b, s, d = x.shape
    tm = min(TM_POSTMIX, s)
    tok = lambda width: pl.BlockSpec((1, tm, width), lambda i, j: (i, j, 0))
    return pl.pallas_call(
        _postmix_kernel,
        grid=(b, s // tm),
        in_specs=[tok(d), tok(a.shape[2]), tok(r.shape[2]),
                  pl.BlockSpec((1, 6, d), lambda i, j: (i, 0, 0)),
                  _const_spec(w["w_o"].shape), _const_spec((1, d))],
        out_specs=[tok(d), tok(d)],
        out_shape=[jax.ShapeDtypeStruct((b, s, d), F32), jax.ShapeDtypeStruct((b, s, d), BF16)],
        compiler_params=_params(("parallel", "parallel")),
        name="postmix",
    )(x, a, r, mod, w["w_o"], w["norm2"])


def _extract_fast(xs, vals_refs):
    def body(r, rems):
        mark = -jnp.asarray(r + 1, F32) * MARK
        out = []
        for rem, vals_ref in zip(rems, vals_refs):
            m = jnp.max(rem, axis=0, keepdims=True)
            vals_ref[pl.ds(r, 1), :] = m
            out.append(jnp.where(rem == m, mark, rem))
        return tuple(out)

    return lax.fori_loop(0, PEER_TOPK, body, tuple(xs))


def _extract_exact(x, order, vals_ref):
    def body(r, rem):
        m = jnp.max(rem, axis=0, keepdims=True)
        vals_ref[pl.ds(r, 1), :] = m
        first = jnp.min(jnp.where(rem == m, order, ORDER_END), axis=0, keepdims=True)
        return jnp.where(order == first, -jnp.asarray(r + 1, F32) * MARK, rem)

    return lax.fori_loop(0, PEER_TOPK, body, x)


def _marked(rem):
    return (rem <= -0.5 * MARK) & (rem > 0.5 * NOT_A_CANDIDATE)


def _count_marked(rem):
    return jnp.sum(jnp.where(_marked(rem), 1.0, 0.0), axis=0, keepdims=True)


def _bf16_twice(x):
    hi = pltpu.bitcast(x.astype(BF16).astype(F32), jnp.uint32)
    return hi | (hi >> 16)


def _routing_kernel(h2_ref, wq_ref, keys_ref, meta_ref, grp_ref, rta_ref, rtb_ref,
                    pq_ref, a_ref, b_ref, pv_ref):
    k = PEER_TOPK
    hd = pl.program_id(2)
    tm = h2_ref.shape[1]

    @pl.when(hd == 0)
    def _():
        pq = jnp.dot(h2_ref[0], wq_ref[...], preferred_element_type=F32).astype(BF16)
        for i in range(2 * PEER_HEADS):
            pq_ref[i] = pq[:, i * PEER_DKEY:(i + 1) * PEER_DKEY]

    nt = (((1,), (1,)), ((), ()))
    s1 = lax.dot_general(keys_ref[0, 0], pq_ref[2 * hd], nt, preferred_element_type=F32)
    s2 = lax.dot_general(keys_ref[0, 1], pq_ref[2 * hd + 1], nt, preferred_element_type=F32)

    def route(exact):
        if exact:
            kiota = lax.broadcasted_iota(jnp.int32, (PEER_NKEYS, tm), 0).astype(F32)
            m1 = _extract_exact(s1, kiota, a_ref)
            m2 = _extract_exact(s2, kiota, b_ref)
        else:
            m1, m2 = _extract_fast((s1, s2), (a_ref, b_ref))
        a16 = a_ref[...]
        b16 = b_ref[...]
        blocks = [a16[0:1, :] + b16[0:8, :], a16[0:1, :] + b16[8:16, :]]
        blocks += [a16[r1:r1 + 1, :] + b16[0:8, :] for r1 in range(1, 8)]
        blocks.append(a16[8:16, :] + b16[0:1, :])
        cand = jnp.where(meta_ref[0] > 0.0, jnp.concatenate(blocks, axis=0), NOT_A_CANDIDATE)
        if exact:
            mc = _extract_exact(cand, meta_ref[1], pv_ref)
        else:
            mc, = _extract_fast((cand,), (pv_ref,))
        pv = pv_ref[...]
        z = jnp.sum(jnp.exp(pv - pv[0:1, :]), axis=0, keepdims=True)
        picked = jnp.where(_marked(mc), 1.0, 0.0).astype(BF16)
        counts = jnp.dot(grp_ref[...], picked, preferred_element_type=F32).astype(BF16)
        top1 = m1 <= -0.5 * MARK
        rank1 = jnp.where(top1, m1 * (-1.0 / MARK) - 1.0, NOT_RANKED).astype(BF16)
        n1 = jnp.zeros((PEER_NKEYS, tm), BF16)
        for r1 in range(k):
            n1 = n1 + jnp.where(rank1 == r1, counts[r1:r1 + 1, :], jnp.zeros((), BF16))
        rank2 = jnp.where(m2 <= -0.5 * MARK, m2 * (-1.0 / MARK) - 1.0, NOT_RANKED)
        rta_ref[0, 0, 0] = _bf16_twice(jnp.where(top1, jnp.exp(s1 - a16[0:1, :]), 0.0) / z)
        rta_ref[0, 0, 1] = _bf16_twice(n1)
        rtb_ref[0, 0, 0] = pltpu.bitcast(rank2.astype(BF16), jnp.uint32)
        rtb_ref[0, 0, 1] = pltpu.bitcast(jnp.exp(s2 - b16[0:1, :]).astype(BF16), jnp.uint32)
        return _count_marked(m1) + _count_marked(m2) + _count_marked(mc)

    extracted = route(exact=False)

    @pl.when(jnp.max(extracted) > 3.0 * k)
    def _():
        route(exact=True)


def _candidate_tables(tm):
    import numpy as np
    k = PEER_TOPK
    pairs = [(0, r2) for r2 in range(16)]
    for r1 in range(1, 8):
        pairs += [(r1, r2) for r2 in range(8)]
    pairs += [(r1, 0) for r1 in range(8, 16)]
    valid = np.array([(r1 + 1) * (r2 + 1) <= k for r1, r2 in pairs], np.float32)
    fidx = np.array([r1 * k + r2 for r1, r2 in pairs], np.float32)
    meta = np.stack([valid, np.where(valid > 0, fidx, ORDER_END)])[:, :, None] * np.ones((1, 1, tm), np.float32)
    grp = np.zeros((k, len(pairs)), np.float32)
    for row, (r1, _) in enumerate(pairs):
        grp[r1, row] = valid[row]
    return jnp.asarray(meta, F32), jnp.asarray(grp, BF16)


def _routing(h2, w):
    b, s, d = h2.shape
    tm = min(TM_ROUTE, s)
    k = PEER_TOPK
    assert k == 16 and PEER_NKEYS >= 16
    meta, grp = _candidate_tables(tm)
    out_a = pl.BlockSpec((1, 1, 2, PEER_NKEYS, tm), lambda i, j, h: (i, h, 0, 0, j))
    out_b = pl.BlockSpec((1, 1, 2, PEER_NKEYS // 2, tm), lambda i, j, h: (i, h, 0, 0, j))
    shape_a = (b, PEER_HEADS, 2, PEER_NKEYS, s)
    shape_b = (b, PEER_HEADS, 2, PEER_NKEYS // 2, s)
    return pl.pallas_call(
        _routing_kernel,
        grid=(b, s // tm, PEER_HEADS),
        in_specs=[pl.BlockSpec((1, tm, d), lambda i, j, h: (i, j, 0)),
                  _const_spec(w["peer_wq"].shape),
                  pl.BlockSpec((1, 2, PEER_NKEYS, PEER_DKEY), lambda i, j, h: (h, 0, 0, 0)),
                  _const_spec(meta.shape), _const_spec(grp.shape)],
        out_specs=[out_a, out_b],
        out_shape=[jax.ShapeDtypeStruct(shape_a, jnp.uint32), jax.ShapeDtypeStruct(shape_b, jnp.uint32)],
        scratch_shapes=[pltpu.VMEM((2 * PEER_HEADS, tm, PEER_DKEY), BF16),
                        pltpu.VMEM((k, tm), F32), pltpu.VMEM((k, tm), F32), pltpu.VMEM((k, tm), F32)],
        compiler_params=_params(("parallel", "parallel", "arbitrary")),
        name="peer_routing",
    )(h2, w["peer_wq"], w["peer_keys"], meta, grp)


def _gelu_tanh(x):
    c = 2.0 * math.sqrt(2.0 / math.pi) * math.log2(math.e)
    return x / (1.0 + jnp.exp2(x * (-c - (c * 0.044715) * (x * x))))


def _peer_kernel(h2_ref, u_ref, vt_ref, rta_ref, rtb_ref, po_ref, acc_ref, at_ref, y_ref):
    e = pl.program_id(2)
    ne = pl.num_programs(2)
    tm = h2_ref.shape[1]
    d = h2_ref.shape[2]
    keys_per_part = EH_PEER // PEER_NKEYS
    tc = min(TC_PEER, tm)
    words = PEER_NKEYS // 2

    @pl.when(e == 0)
    def _():
        acc_ref[...] = jnp.zeros_like(acc_ref)

    h2 = h2_ref[0]
    nparts = EB_PEER // EH_PEER
    acc_rows = math.gcd(d, ACC_ROWS)

    def activations(part):
        at_ref[part] = lax.dot_general(u_ref[part * EH_PEER:(part + 1) * EH_PEER, :], h2,
                                       (((1,), (1,)), ((), ())), preferred_element_type=F32)

    def weigh(part):
        for j in range(keys_per_part):
            c = part * keys_per_part + j
            for lc in range(tm // tc):
                lanes = slice(lc * tc, (lc + 1) * tc)
                wgt = None
                for hd in range(PEER_HEADS):
                    p1 = pltpu.bitcast(jnp.broadcast_to(rta_ref[0, hd, 0, c:c + 1, lanes], (8, tc)), BF16)
                    n1 = pltpu.bitcast(jnp.broadcast_to(rta_ref[0, hd, 1, c:c + 1, lanes], (8, tc)), BF16)
                    r2 = pltpu.bitcast(rtb_ref[0, hd, 0, :, lanes], BF16).reshape(words // 8, BF16_ROWS, tc)
                    p2 = pltpu.bitcast(rtb_ref[0, hd, 1, :, lanes], BF16).reshape(words // 8, BF16_ROWS, tc)
                    w = jnp.where(r2 < n1[None], p1[None] * p2, jnp.zeros((), BF16))
                    wgt = w if wgt is None else wgt + w
                g = _gelu_tanh(at_ref[part, j * PEER_NKEYS:(j + 1) * PEER_NKEYS, lanes]).astype(BF16)
                y_ref[part, j * words:(j + 1) * words, lanes] = pltpu.bitcast(wgt.reshape(PEER_NKEYS, tc) * g,
                                                                              jnp.uint32)

    def accumulate(part):
        for rb in range(d // acc_rows):
            rows = slice(rb * acc_rows, (rb + 1) * acc_rows)
            acc_ref[rows, :] += jnp.dot(vt_ref[rows, part * EH_PEER:(part + 1) * EH_PEER],
                                        pltpu.bitcast(y_ref[part], BF16), preferred_element_type=F32)

    activations(0)
    if nparts > 1:
        activations(1)
    for part in range(nparts):
        if part >= 1:
            accumulate(part - 1)
        weigh(part)
        if part + 2 < nparts:
            activations(part + 2)
    accumulate(nparts - 1)

    @pl.when(e == ne - 1)
    def _():
        po_ref[0] = acc_ref[...].T.astype(BF16)


def _peer(h2, rta, rtb, w):
    b, s, d = h2.shape
    tm = min(TM_PEER, s)
    keys_per_step = EB_PEER // PEER_NKEYS
    assert keys_per_step == 8 and EB_PEER % EH_PEER == 0 and EH_PEER % PEER_NKEYS == 0
    ne = w["peer_u"].shape[0] // EB_PEER
    once = pl.Buffered(1)
    tok = lambda **kw: pl.BlockSpec((1, tm, d), lambda i, j, e: (i, j, 0), **kw)
    return pl.pallas_call(
        _peer_kernel,
        grid=(b, s // tm, ne),
        in_specs=[tok(pipeline_mode=once),
                  pl.BlockSpec((EB_PEER, d), lambda i, j, e: (e, 0)),
                  pl.BlockSpec((d, EB_PEER), lambda i, j, e: (0, e)),
                  pl.BlockSpec((1, PEER_HEADS, 2, keys_per_step, tm), lambda i, j, e: (i, 0, 0, e, j)),
                  pl.BlockSpec((1, PEER_HEADS, 2, PEER_NKEYS // 2, tm), lambda i, j, e: (i, 0, 0, 0, j),
                               pipeline_mode=once)],
        out_specs=tok(),
        out_shape=jax.ShapeDtypeStruct((b, s, d), BF16),
        scratch_shapes=[pltpu.VMEM((d, tm), F32), pltpu.VMEM((EB_PEER // EH_PEER, EH_PEER, tm), F32),
                        pltpu.VMEM((EB_PEER // EH_PEER, EH_PEER // 2, tm), jnp.uint32)],
        compiler_params=_params(("parallel", "parallel", "arbitrary")),
        name="peer_experts",
    )(h2, w["peer_u"], w["peer_vt"], rta, rtb)


def _final_kernel(x1_ref, po_ref, mod_ref, y_ref):
    y_ref[0] = x1_ref[0] + mod_ref[0, 5:6, :] * po_ref[0].astype(F32)


def _final(x1, po, mod):
    b, s, d = x1.shape
    tm = min(TM_FINAL, s)
    tok = lambda: pl.BlockSpec((1, tm, d), lambda i, j: (i, j, 0))
    return pl.pallas_call(
        _final_kernel,
        grid=(b, s // tm),
        in_specs=[tok(), tok(), pl.BlockSpec((1, 6, d), lambda i, j: (i, 0, 0))],
        out_specs=tok(),
        out_shape=jax.ShapeDtypeStruct((b, s, d), F32),
        compiler_params=_params(("parallel", "parallel")),
        name="peer_residual",
    )(x1, po, mod)


def _prepare_weights(norm1_w, norm2_w, w_in, q_a_norm, kv_a_norm, w_uq, w_uk, w_uv, q_norm, k_norm,
                     ret_decay_logit, ret_gn_w, w_o, peer_wq, peer_sub_keys, peer_u, peer_v):
    half = MLA_ROPE // 2

    def rope_lanes(t):
        z = jnp.zeros(t.shape[:-1] + (half,), t.dtype)
        return jnp.concatenate([t[..., :half], z, t[..., half:], z], axis=-1)

    def head_lanes(t):
        return jnp.concatenate([t[..., :MLA_NOPE], rope_lanes(t[..., MLA_NOPE:])], axis=-1)

    w_in = w_in[0]
    o = Q_LORA + KV_LORA
    w_in_p = jnp.concatenate([w_in[:, :o], rope_lanes(w_in[:, o:o + MLA_ROPE]), w_in[:, o + MLA_ROPE:]], axis=1)
    w_uq_p = head_lanes(w_uq[0].reshape(Q_LORA, MLA_HEADS, MLA_QK)).reshape(Q_LORA, MLA_HEADS * HEAD_PAD)
    row = lambda t: t.reshape(1, -1).astype(F32)
    return {
        "norm1": row(norm1_w[0]), "norm2": row(norm2_w[0]),
        "w_in": w_in_p.astype(BF16),
        "q_a_norm": row(q_a_norm[0]), "kv_a_norm": row(kv_a_norm[0]),
        "w_uq": w_uq_p.astype(BF16), "w_uk": w_uk[0].astype(BF16), "w_uv": w_uv[0].astype(BF16),
        "q_norm": row(head_lanes(q_norm[0])), "k_norm": row(head_lanes(k_norm[0])),
        "ret_logit": ret_decay_logit[0].T.reshape(RET_HEADS, 2, 1).astype(F32),
        "ret_gn": row(ret_gn_w[0]),
        "w_o": w_o[0].astype(BF16),
        "peer_wq": peer_wq[0].astype(BF16),
        "peer_keys": peer_sub_keys[0].astype(BF16),
        "peer_u": peer_u[0].astype(BF16),
        "peer_vt": peer_v[0].T.astype(BF16),
    }


def _trunk(x, mod, w):
    q, k, v, rq, rk, rv, rg = _premix(x, mod, w)
    a = _attention(q, k, v)
    r = _retention(rq, rk, rv, rg, w)
    x1, h2 = _postmix(x, a, r, mod, w)
    rta, rtb = _routing(h2, w)
    po = _peer(h2, rta, rtb, w)
    return _final(x1, po, mod)


def kernel(x_prompt, x_sample, c_prompt, c_sample, norm1_w, norm2_w, w_ada, b_ada, w_in, q_a_norm, kv_a_norm, w_uq, w_uk, w_uv, q_norm, k_norm, ret_decay_logit, ret_gn_w, w_o, peer_wq, peer_sub_keys, peer_u, peer_v):
    assert w_ada.shape[0] == 1, "single-layer trunk"
    d = x_prompt.shape[-1]
    w = _prepare_weights(norm1_w, norm2_w, w_in, q_a_norm, kv_a_norm, w_uq, w_uk, w_uv, q_norm, k_norm,
                         ret_decay_logit, ret_gn_w, w_o, peer_wq, peer_sub_keys, peer_u, peer_v)
    nbp = c_prompt.shape[0]
    mod = _adaln(jnp.concatenate([c_prompt, c_sample], axis=0), w_ada[0], b_ada[0]).reshape(-1, 6, d)
    y_prompt = _trunk(x_prompt, mod[:nbp], w)
    y_sample = _trunk(x_sample, mod[nbp:], w)
    return (y_prompt, y_sample)
```

```python
import math

import jax
import jax.numpy as jnp
from jax import lax
from jax.experimental import pallas as pl
from jax.experimental.pallas import tpu as pltpu

F32 = jnp.float32
BF16 = jnp.bfloat16

MLA_HEADS = 8
MLA_NOPE = 128
MLA_ROPE = 64
MLA_QK = MLA_NOPE + MLA_ROPE
MLA_V = 128
Q_LORA = 512
KV_LORA = 256
RET_HEADS = 8
RET_DK = 128
RET_DV = 128
RET_CHUNK = 128
PEER_HEADS = 8
PEER_NKEYS = 128
PEER_DKEY = 128
PEER_TOPK = 16
ROPE_BASE = 10000.0
EPS = 1e-6

LANE = 128
HEAD_PAD = 2 * LANE
NOT_RANKED = 99.0
MARK = 2.0 ** 100
NOT_A_CANDIDATE = -(2.0 ** 120)
ORDER_END = 1e9
VMEM_LIMIT = 56 * 1024 * 1024

TM_PREMIX = 256
TQ_ATTN = 1024
TQ_SUB = 256
HB_RET = 4
TM_POSTMIX = 256
TM_ROUTE = 512
TM_PEER = 1024
EB_PEER = 1024
EH_PEER = 512
ACC_ROWS = 512
BF16_ROWS = 16
TC_PEER = 256
TM_FINAL = 512


def _params(sem):
    return pltpu.CompilerParams(dimension_semantics=sem, vmem_limit_bytes=VMEM_LIMIT)


def _const_spec(shape):
    nd = len(shape)
    return pl.BlockSpec(shape, lambda *_: (0,) * nd, pipeline_mode=pl.Buffered(1))


def _adaln_kernel(c_ref, w_ref, b_ref, o_ref):
    c = c_ref[...]
    sc = c / (1.0 + jnp.exp(-c))
    o_ref[...] = jnp.dot(sc, w_ref[...], preferred_element_type=F32,
                         precision=lax.Precision.HIGHEST) + b_ref[...]


def _adaln(c, w_ada, b_ada):
    nb, d = c.shape
    n = w_ada.shape[1]
    tn = math.gcd(n, 1024)
    return pl.pallas_call(
        _adaln_kernel,
        grid=(n // tn,),
        in_specs=[pl.BlockSpec((nb, d), lambda j: (0, 0)),
                  pl.BlockSpec((d, tn), lambda j: (0, j)),
                  pl.BlockSpec((1, tn), lambda j: (0, j))],
        out_specs=pl.BlockSpec((nb, tn), lambda j: (0, j)),
        out_shape=jax.ShapeDtypeStruct((nb, n), F32),
        compiler_params=_params(("arbitrary",)),
        name="adaln",
    )(c, w_ada, b_ada.reshape(1, n))


def _rms(x, w):
    ms = jnp.mean(x * x, axis=-1, keepdims=True)
    return x * lax.rsqrt(ms + EPS) * w


def _rot(x, c, s):
    return x * c + pltpu.roll(x, 64, 1) * s


def _premix_kernel(x_ref, mod_ref, n1_ref, win_ref, qan_ref, kvan_ref, wuq_ref, qn_ref,
                   wuk_ref, wuv_ref, kn_ref, cm_ref, sm_ref, cr_ref, sr_ref,
                   q_ref, k_ref, v_ref, rq_ref, rk_ref, rv_ref, rg_ref):
    x = x_ref[0]
    shift1 = mod_ref[0, 0:1, :]
    scale1 = mod_ref[0, 1:2, :]
    h = (_rms(x, n1_ref[...]) * (1.0 + scale1) + shift1).astype(BF16)

    o_ckv = Q_LORA
    o_kr = o_ckv + KV_LORA
    o_rq = o_kr + LANE
    hk = RET_HEADS * RET_DK
    hv = RET_HEADS * RET_DV
    o_rk = o_rq + hk
    o_rv = o_rk + hk
    o_rg = o_rv + hv

    def proj(a, b):
        return jnp.dot(h, win_ref[:, a:b], preferred_element_type=F32)

    cm = cm_ref[...]
    sm = sm_ref[...]
    att_scale = MLA_QK ** -0.5 * math.log2(math.e)

    cq = _rms(proj(0, o_ckv), qan_ref[...]).astype(BF16)
    q_raw = jnp.dot(cq, wuq_ref[...], preferred_element_type=F32)
    qn = qn_ref[...]
    for hd in range(MLA_HEADS):
        qh = q_raw[:, hd * HEAD_PAD:(hd + 1) * HEAD_PAD]
        r = lax.rsqrt(jnp.sum(qh * qh, axis=-1, keepdims=True) * (1.0 / MLA_QK) + EPS)
        qh = qh * (r * att_scale) * qn
        q_ref[0, :, hd * HEAD_PAD:hd * HEAD_PAD + LANE] = qh[:, :LANE].astype(BF16)
        q_ref[0, :, hd * HEAD_PAD + LANE:(hd + 1) * HEAD_PAD] = _rot(qh[:, LANE:], cm, sm).astype(BF16)

    ckv = _rms(proj(o_ckv, o_kr), kvan_ref[...]).astype(BF16)
    k_nope = jnp.dot(ckv, wuk_ref[...], preferred_element_type=F32)
    v_ref[0] = jnp.dot(ckv, wuv_ref[...], preferred_element_type=F32).astype(BF16)
    kr = proj(o_kr, o_rq)
    ss_r = jnp.sum(kr * kr, axis=-1, keepdims=True)
    kn = kn_ref[...]
    kr_rot = _rot(kr * kn[:, LANE:], cm, sm)
    for hd in range(MLA_HEADS):
        kh = k_nope[:, hd * MLA_NOPE:(hd + 1) * MLA_NOPE]
        r = lax.rsqrt((jnp.sum(kh * kh, axis=-1, keepdims=True) + ss_r) * (1.0 / MLA_QK) + EPS)
        k_ref[0, :, hd * HEAD_PAD:hd * HEAD_PAD + LANE] = (kh * r * kn[:, :LANE]).astype(BF16)
        k_ref[0, :, hd * HEAD_PAD + LANE:(hd + 1) * HEAD_PAD] = (kr_rot * r).astype(BF16)

    cr = cr_ref[...]
    sr = sr_ref[...]
    rq = proj(o_rq, o_rk)
    rk = proj(o_rk, o_rv)
    k_scale = RET_DK ** -0.5
    for hd in range(RET_HEADS):
        sl = slice(hd * RET_DK, (hd + 1) * RET_DK)
        rq_ref[0, :, sl] = _rot(rq[:, sl], cr, sr).astype(BF16)
        rk_ref[0, :, sl] = (_rot(rk[:, sl], cr, sr) * k_scale).astype(BF16)
    rv_ref[0] = proj(o_rv, o_rg).astype(BF16)
    rg_ref[0] = proj(o_rg, o_rg + hv).astype(BF16)


def _premix(x, mod, w):
    b, s, d = x.shape
    tm = min(TM_PREMIX, s)
    tok = lambda width: pl.BlockSpec((1, tm, width), lambda i, j: (i, j, 0))
    tab = pl.BlockSpec((tm, LANE), lambda i, j: (j, 0))
    cm, sm, cr, sr = _rope_tables(s)
    hq = MLA_HEADS * HEAD_PAD
    hv = MLA_HEADS * MLA_V
    rk = RET_HEADS * RET_DK
    rv = RET_HEADS * RET_DV
    out_widths = (hq, hq, hv, rk, rk, rv, rv)
    return pl.pallas_call(
        _premix_kernel,
        grid=(b, s // tm),
        in_specs=[tok(d), pl.BlockSpec((1, 6, d), lambda i, j: (i, 0, 0)),
                  _const_spec((1, d)), _const_spec(w["w_in"].shape),
                  _const_spec((1, Q_LORA)), _const_spec((1, KV_LORA)),
                  _const_spec(w["w_uq"].shape), _const_spec((1, HEAD_PAD)),
                  _const_spec(w["w_uk"].shape), _const_spec(w["w_uv"].shape),
                  _const_spec((1, HEAD_PAD)), tab, tab, tab, tab],
        out_specs=[tok(n) for n in out_widths],
        out_shape=[jax.ShapeDtypeStruct((b, s, n), BF16) for n in out_widths],
        compiler_params=_params(("parallel", "parallel")),
        name="premix",
    )(x, mod, w["norm1"], w["w_in"], w["q_a_norm"], w["kv_a_norm"], w["w_uq"], w["q_norm"],
      w["w_uk"], w["w_uv"], w["k_norm"], cm, sm, cr, sr)


def _rope_tables(s):
    pos = jnp.arange(s, dtype=F32)[:, None]

    def cs(dim):
        inv = 1.0 / (ROPE_BASE ** (jnp.arange(0, dim, 2, dtype=F32) / dim))
        ang = pos * inv[None, :]
        return jnp.cos(ang), jnp.sin(ang)

    c, sn = cs(MLA_ROPE)
    z = jnp.zeros_like(c)
    cm = jnp.concatenate([c, z, c, z], axis=-1)
    sm = jnp.concatenate([-sn, z, sn, z], axis=-1)
    c, sn = cs(RET_DK)
    cr = jnp.concatenate([c, c], axis=-1)
    sr = jnp.concatenate([-sn, sn], axis=-1)
    return cm, sm, cr, sr


def _attn_kernel(q_ref, k_ref, v_ref, o_ref):
    k = k_ref[0]
    v = v_ref[0]
    tq = q_ref.shape[1]
    for r in range(tq // TQ_SUB):
        rows = slice(r * TQ_SUB, (r + 1) * TQ_SUB)
        s = lax.dot_general(q_ref[0, rows, :], k, (((1,), (1,)), ((), ())), preferred_element_type=F32)
        m = jnp.max(s, axis=-1, keepdims=True)
        p = jnp.exp2(s - m)
        l = jnp.sum(p, axis=-1, keepdims=True)
        o = jnp.dot(p.astype(BF16), v, preferred_element_type=F32)
        o_ref[0, rows, :] = (o / l).astype(BF16)


def _attention(q, k, v):
    b, s, _ = q.shape
    tq = min(TQ_ATTN, s)
    assert tq % TQ_SUB == 0
    return pl.pallas_call(
        _attn_kernel,
        grid=(b, MLA_HEADS, s // tq),
        in_specs=[pl.BlockSpec((1, tq, HEAD_PAD), lambda i, h, j: (i, j, h)),
                  pl.BlockSpec((1, s, HEAD_PAD), lambda i, h, j: (i, 0, h)),
                  pl.BlockSpec((1, s, MLA_V), lambda i, h, j: (i, 0, h))],
        out_specs=pl.BlockSpec((1, tq, MLA_V), lambda i, h, j: (i, j, h)),
        out_shape=jax.ShapeDtypeStruct((b, s, MLA_HEADS * MLA_V), BF16),
        compiler_params=_params(("parallel", "parallel", "arbitrary")),
        name="attention",
    )(q, k, v)


def _retention_kernel(lg_ref, q_ref, k_ref, v_ref, g_ref, gn_ref, o_ref,
                      of_ref, cb_ref, dec_ref, sf_ref, sb_ref):
    c = RET_CHUNK
    nc = q_ref.shape[1] // c
    hb = q_ref.shape[2] // RET_DK
    row = lax.broadcasted_iota(jnp.int32, (c, c), 0).astype(F32)
    col = lax.broadcasted_iota(jnp.int32, (c, c), 1).astype(F32)
    rel = row - col
    pos = lax.broadcasted_iota(jnp.int32, (c, 1), 0).astype(F32)
    vec = []
    for hh in range(hb):
        logit = lg_ref[hh]
        lg = jnp.minimum(logit, 0.0) - jnp.log1p(jnp.exp(-jnp.abs(logit)))
        lg_f = lg[0:1, :]
        lg_b = lg[1:2, :]
        dec_ref[hh] = jnp.where(rel >= 0, jnp.exp(lg_f * jnp.maximum(rel, 0.0)),
                                jnp.exp(lg_b * jnp.maximum(-rel, 0.0)))
        vec.append(dict(qd_f=jnp.exp(lg_f * (pos + 1.0)), kd_f=jnp.exp(lg_f * (c - 1.0 - pos)),
                        qd_b=jnp.exp(lg_b * (c - pos)), kd_b=jnp.exp(lg_b * pos),
                        cd_f=jnp.exp(lg_f * float(c)), cd_b=jnp.exp(lg_b * float(c))))
    sf_ref[...] = jnp.zeros_like(sf_ref)
    sb_ref[...] = jnp.zeros_like(sb_ref)

    def rows(n):
        return pl.ds(pl.multiple_of(n * c, c), c)

    def kv_update(s_ref, hh, kn, vn, kd, cd):
        kt = (kn.astype(F32) * kd).T.astype(BF16)
        s_ref[hh] = s_ref[hh] * cd + jnp.dot(kt, vn, preferred_element_type=F32)

    def scan(i, carry):
        n = i
        m = nc - 1 - i
        for hh in range(hb):
            lanes = slice(hh * RET_DK, (hh + 1) * RET_DK)
            t = vec[hh]
            qn = q_ref[0, rows(n), lanes]
            kn = k_ref[0, rows(n), lanes]
            vn = v_ref[0, rows(n), lanes]
            sc = lax.dot_general(qn, kn, (((1,), (1,)), ((), ())), preferred_element_type=F32) * dec_ref[hh]
            o = jnp.dot(sc.astype(BF16), vn, preferred_element_type=F32)
            o = o + jnp.dot((qn.astype(F32) * t["qd_f"]).astype(BF16), sf_ref[hh].astype(BF16),
                            preferred_element_type=F32)
            of_ref[rows(n), lanes] = o
            kv_update(sf_ref, hh, kn, vn, t["kd_f"], t["cd_f"])
            qm = q_ref[0, rows(m), lanes]
            cb_ref[rows(m), lanes] = jnp.dot((qm.astype(F32) * t["qd_b"]).astype(BF16), sb_ref[hh].astype(BF16),
                                             preferred_element_type=F32)
            kv_update(sb_ref, hh, k_ref[0, rows(m), lanes], v_ref[0, rows(m), lanes], t["kd_b"], t["cd_b"])
        return carry

    lax.fori_loop(0, nc, scan, 0, unroll=4)

    def finish(n, carry):
        for hh in range(hb):
            lanes = slice(hh * RET_DV, (hh + 1) * RET_DV)
            o = of_ref[rows(n), lanes] + cb_ref[rows(n), lanes]
            mu = jnp.mean(o, axis=-1, keepdims=True)
            dv = o - mu
            var = jnp.mean(dv * dv, axis=-1, keepdims=True)
            on = dv * lax.rsqrt(var + EPS) * gn_ref[:, lanes]
            g = g_ref[0, rows(n), lanes].astype(F32)
            o_ref[0, rows(n), lanes] = (g / (1.0 + jnp.exp(-g)) * on).astype(BF16)
        return carry

    lax.fori_loop(0, nc, finish, 0)


def _retention(rq, rk, rv, rg, w):
    b, s, _ = rq.shape
    hb = min(HB_RET, RET_HEADS)
    assert RET_HEADS % hb == 0 and RET_DK == RET_DV
    hs = lambda **kw: pl.BlockSpec((1, s, hb * RET_DK), lambda i, h: (i, 0, h), **kw)
    once = pl.Buffered(1)
    return pl.pallas_call(
        _retention_kernel,
        grid=(b, RET_HEADS // hb),
        in_specs=[pl.BlockSpec((hb, 2, 1), lambda i, h: (h, 0, 0)),
                  hs(pipeline_mode=once), hs(pipeline_mode=once), hs(pipeline_mode=once), hs(pipeline_mode=once),
                  pl.BlockSpec((1, hb * RET_DV), lambda i, h: (0, h))],
        out_specs=hs(),
        out_shape=jax.ShapeDtypeStruct((b, s, RET_HEADS * RET_DV), BF16),
        scratch_shapes=[pltpu.VMEM((s, hb * RET_DV), F32), pltpu.VMEM((s, hb * RET_DV), F32),
                        pltpu.VMEM((hb, RET_CHUNK, RET_CHUNK), F32),
                        pltpu.VMEM((hb, RET_DK, RET_DV), F32), pltpu.VMEM((hb, RET_DK, RET_DV), F32)],
        compiler_params=_params(("parallel", "parallel")),
        name="retention",
    )(w["ret_logit"], rq, rk, rv, rg, w["ret_gn"])


def _postmix_kernel(x_ref, a_ref, r_ref, mod_ref, wo_ref, n2_ref, x1_ref, h2_ref):
    na = a_ref.shape[2]
    mix = jnp.dot(a_ref[0], wo_ref[:na, :], preferred_element_type=F32)
    mix = mix + jnp.dot(r_ref[0], wo_ref[na:, :], preferred_element_type=F32)
    x1 = x_ref[0] + mod_ref[0, 2:3, :] * mix
    x1_ref[0] = x1
    h2 = _rms(x1, n2_ref[...]) * (1.0 + mod_ref[0, 4:5, :]) + mod_ref[0, 3:4, :]
    h2_ref[0] = h2.astype(BF16)


def _postmix(x, a, r, mod, w):
    b, s, d = x.shape
    tm = min(TM_POSTMIX, s)
    tok = lambda width: pl.BlockSpec((1, tm, width), lambda i, j: (i, j, 0))
    return pl.pallas_call(
        _postmix_kernel,
        grid=(b, s // tm),
        in_specs=[tok(d), tok(a.shape[2]), tok(r.shape[2]),
                  pl.BlockSpec((1, 6, d), lambda i, j: (i, 0, 0)),
                  _const_spec(w["w_o"].shape), _const_spec((1, d))],
        out_specs=[tok(d), tok(d)],
        out_shape=[jax.ShapeDtypeStruct((b, s, d), F32), jax.ShapeDtypeStruct((b, s, d), BF16)],
        compiler_params=_params(("parallel", "parallel")),
        name="postmix",
    )(x, a, r, mod, w["w_o"], w["norm2"])


def _extract_fast(xs, vals_refs):
    def body(r, rems):
        mark = -jnp.asarray(r + 1, F32) * MARK
        out = []
        for rem, vals_ref in zip(rems, vals_refs):
            m = jnp.max(rem, axis=0, keepdims=True)
            vals_ref[pl.ds(r, 1), :] = m
            out.append(jnp.where(rem == m, mark, rem))
        return tuple(out)

    return lax.fori_loop(0, PEER_TOPK, body, tuple(xs))


def _extract_exact(x, order, vals_ref):
    def body(r, rem):
        m = jnp.max(rem, axis=0, keepdims=True)
        vals_ref[pl.ds(r, 1), :] = m
        first = jnp.min(jnp.where(rem == m, order, ORDER_END), axis=0, keepdims=True)
        return jnp.where(order == first, -jnp.asarray(r + 1, F32) * MARK, rem)

    return lax.fori_loop(0, PEER_TOPK, body, x)


def _marked(rem):
    return (rem <= -0.5 * MARK) & (rem > 0.5 * NOT_A_CANDIDATE)


def _count_marked(rem):
    return jnp.sum(jnp.where(_marked(rem), 1.0, 0.0), axis=0, keepdims=True)


def _bf16_twice(x):
    hi = pltpu.bitcast(x.astype(BF16).astype(F32), jnp.uint32)
    return hi | (hi >> 16)


def _routing_kernel(h2_ref, wq_ref, keys_ref, meta_ref, grp_ref, rta_ref, rtb_ref,
                    pq_ref, a_ref, b_ref, pv_ref):
    k = PEER_TOPK
    hd = pl.program_id(2)
    tm = h2_ref.shape[1]

    @pl.when(hd == 0)
    def _():
        pq = jnp.dot(h2_ref[0], wq_ref[...], preferred_element_type=F32).astype(BF16)
        for i in range(2 * PEER_HEADS):
            pq_ref[i] = pq[:, i * PEER_DKEY:(i + 1) * PEER_DKEY]

    nt = (((1,), (1,)), ((), ()))
    s1 = lax.dot_general(keys_ref[0, 0], pq_ref[2 * hd], nt, preferred_element_type=F32)
    s2 = lax.dot_general(keys_ref[0, 1], pq_ref[2 * hd + 1], nt, preferred_element_type=F32)

    def route(exact):
        if exact:
            kiota = lax.broadcasted_iota(jnp.int32, (PEER_NKEYS, tm), 0).astype(F32)
            m1 = _extract_exact(s1, kiota, a_ref)
            m2 = _extract_exact(s2, kiota, b_ref)
        else:
            m1, m2 = _extract_fast((s1, s2), (a_ref, b_ref))
        a16 = a_ref[...]
        b16 = b_ref[...]
        blocks = [a16[0:1, :] + b16[0:8, :], a16[0:1, :] + b16[8:16, :]]
        blocks += [a16[r1:r1 + 1, :] + b16[0:8, :] for r1 in range(1, 8)]
        blocks.append(a16[8:16, :] + b16[0:1, :])
        cand = jnp.where(meta_ref[0] > 0.0, jnp.concatenate(blocks, axis=0), NOT_A_CANDIDATE)
        if exact:
            mc = _extract_exact(cand, meta_ref[1], pv_ref)
        else:
            mc, = _extract_fast((cand,), (pv_ref,))
        pv = pv_ref[...]
        z = jnp.sum(jnp.exp(pv - pv[0:1, :]), axis=0, keepdims=True)
        picked = jnp.where(_marked(mc), 1.0, 0.0).astype(BF16)
        counts = jnp.dot(grp_ref[...], picked, preferred_element_type=F32).astype(BF16)
        top1 = m1 <= -0.5 * MARK
        rank1 = jnp.where(top1, m1 * (-1.0 / MARK) - 1.0, NOT_RANKED).astype(BF16)
        n1 = jnp.zeros((PEER_NKEYS, tm), BF16)
        for r1 in range(k):
            n1 = n1 + jnp.where(rank1 == r1, counts[r1:r1 + 1, :], jnp.zeros((), BF16))
        rank2 = jnp.where(m2 <= -0.5 * MARK, m2 * (-1.0 / MARK) - 1.0, NOT_RANKED)
        rta_ref[0, 0, 0] = _bf16_twice(jnp.where(top1, jnp.exp(s1 - a16[0:1, :]), 0.0) / z)
        rta_ref[0, 0, 1] = _bf16_twice(n1)
        rtb_ref[0, 0, 0] = pltpu.bitcast(rank2.astype(BF16), jnp.uint32)
        rtb_ref[0, 0, 1] = pltpu.bitcast(jnp.exp(s2 - b16[0:1, :]).astype(BF16), jnp.uint32)
        return _count_marked(m1) + _count_marked(m2) + _count_marked(mc)

    extracted = route(exact=False)

    @pl.when(jnp.max(extracted) > 3.0 * k)
    def _():
        route(exact=True)


def _candidate_tables(tm):
    import numpy as np
    k = PEER_TOPK
    pairs = [(0, r2) for r2 in range(16)]
    for r1 in range(1, 8):
        pairs += [(r1, r2) for r2 in range(8)]
    pairs += [(r1, 0) for r1 in range(8, 16)]
    valid = np.array([(r1 + 1) * (r2 + 1) <= k for r1, r2 in pairs], np.float32)
    fidx = np.array([r1 * k + r2 for r1, r2 in pairs], np.float32)
    meta = np.stack([valid, np.where(valid > 0, fidx, ORDER_END)])[:, :, None] * np.ones((1, 1, tm), np.float32)
    grp = np.zeros((k, len(pairs)), np.float32)
    for row, (r1, _) in enumerate(pairs):
        grp[r1, row] = valid[row]
    return jnp.asarray(meta, F32), jnp.asarray(grp, BF16)


def _routing(h2, w):
    b, s, d = h2.shape
    tm = min(TM_ROUTE, s)
    k = PEER_TOPK
    assert k == 16 and PEER_NKEYS >= 16
    meta, grp = _candidate_tables(tm)
    out_a = pl.BlockSpec((1, 1, 2, PEER_NKEYS, tm), lambda i, j, h: (i, h, 0, 0, j))
    out_b = pl.BlockSpec((1, 1, 2, PEER_NKEYS // 2, tm), lambda i, j, h: (i, h, 0, 0, j))
    shape_a = (b, PEER_HEADS, 2, PEER_NKEYS, s)
    shape_b = (b, PEER_HEADS, 2, PEER_NKEYS // 2, s)
    return pl.pallas_call(
        _routing_kernel,
        grid=(b, s // tm, PEER_HEADS),
        in_specs=[pl.BlockSpec((1, tm, d), lambda i, j, h: (i, j, 0)),
                  _const_spec(w["peer_wq"].shape),
                  pl.BlockSpec((1, 2, PEER_NKEYS, PEER_DKEY), lambda i, j, h: (h, 0, 0, 0)),
                  _const_spec(meta.shape), _const_spec(grp.shape)],
        out_specs=[out_a, out_b],
        out_shape=[jax.ShapeDtypeStruct(shape_a, jnp.uint32), jax.ShapeDtypeStruct(shape_b, jnp.uint32)],
        scratch_shapes=[pltpu.VMEM((2 * PEER_HEADS, tm, PEER_DKEY), BF16),
                        pltpu.VMEM((k, tm), F32), pltpu.VMEM((k, tm), F32), pltpu.VMEM((k, tm), F32)],
        compiler_params=_params(("parallel", "parallel", "arbitrary")),
        name="peer_routing",
    )(h2, w["peer_wq"], w["peer_keys"], meta, grp)


def _gelu_tanh(x):
    c = 2.0 * math.sqrt(2.0 / math.pi) * math.log2(math.e)
    return x / (1.0 + jnp.exp2(x * (-c - (c * 0.044715) * (x * x))))


def _peer_kernel(h2_ref, u_ref, vt_ref, rta_ref, rtb_ref, po_ref, acc_ref, at_ref, y_ref):
    e = pl.program_id(2)
    ne = pl.num_programs(2)
    tm = h2_ref.shape[1]
    d = h2_ref.shape[2]
    keys_per_part = EH_PEER // PEER_NKEYS
    tc = min(TC_PEER, tm)
    words = PEER_NKEYS // 2

    @pl.when(e == 0)
    def _():
        acc_ref[...] = jnp.zeros_like(acc_ref)

    h2 = h2_ref[0]
    nparts = EB_PEER // EH_PEER
    acc_rows = math.gcd(d, ACC_ROWS)

    def activations(part):
        at_ref[part] = lax.dot_general(u_ref[part * EH_PEER:(part + 1) * EH_PEER, :], h2,
                                       (((1,), (1,)), ((), ())), preferred_element_type=F32)

    def weigh(part):
        for j in range(keys_per_part):
            c = part * keys_per_part + j
            for lc in range(tm // tc):
                lanes = slice(lc * tc, (lc + 1) * tc)
                wgt = None
                for hd in range(PEER_HEADS):
                    p1 = pltpu.bitcast(jnp.broadcast_to(rta_ref[0, hd, 0, c:c + 1, lanes], (8, tc)), BF16)
                    n1 = pltpu.bitcast(jnp.broadcast_to(rta_ref[0, hd, 1, c:c + 1, lanes], (8, tc)), BF16)
                    r2 = pltpu.bitcast(rtb_ref[0, hd, 0, :, lanes], BF16).reshape(words // 8, BF16_ROWS, tc)
                    p2 = pltpu.bitcast(rtb_ref[0, hd, 1, :, lanes], BF16).reshape(words // 8, BF16_ROWS, tc)
                    w = jnp.where(r2 < n1[None], p1[None] * p2, jnp.zeros((), BF16))
                    wgt = w if wgt is None else wgt + w
                g = _gelu_tanh(at_ref[part, j * PEER_NKEYS:(j + 1) * PEER_NKEYS, lanes]).astype(BF16)
                y_ref[part, j * words:(j + 1) * words, lanes] = pltpu.bitcast(wgt.reshape(PEER_NKEYS, tc) * g,
                                                                              jnp.uint32)

    def accumulate(part):
        for rb in range(d // acc_rows):
            rows = slice(rb * acc_rows, (rb + 1) * acc_rows)
            acc_ref[rows, :] += jnp.dot(vt_ref[rows, part * EH_PEER:(part + 1) * EH_PEER],
                                        pltpu.bitcast(y_ref[part], BF16), preferred_element_type=F32)

    activations(0)
    if nparts > 1:
        activations(1)
    for part in range(nparts):
        if part >= 1:
            accumulate(part - 1)
        weigh(part)
        if part + 2 < nparts:
            activations(part + 2)
    accumulate(nparts - 1)

    @pl.when(e == ne - 1)
    def _():
        po_ref[0] = acc_ref[...].T.astype(BF16)


def _peer(h2, rta, rtb, w):
    b, s, d = h2.shape
    tm = min(TM_PEER, s)
    keys_per_step = EB_PEER // PEER_NKEYS
    assert keys_per_step == 8 and EB_PEER % EH_PEER == 0 and EH_PEER % PEER_NKEYS == 0
    ne = w["peer_u"].shape[0] // EB_PEER
    once = pl.Buffered(1)
    tok = lambda **kw: pl.BlockSpec((1, tm, d), lambda i, j, e: (i, j, 0), **kw)
    return pl.pallas_call(
        _peer_kernel,
        grid=(b, s // tm, ne),
        in_specs=[tok(pipeline_mode=once),
                  pl.BlockSpec((EB_PEER, d), lambda i, j, e: (e, 0)),
                  pl.BlockSpec((d, EB_PEER), lambda i, j, e: (0, e)),
                  pl.BlockSpec((1, PEER_HEADS, 2, keys_per_step, tm), lambda i, j, e: (i, 0, 0, e, j)),
                  pl.BlockSpec((1, PEER_HEADS, 2, PEER_NKEYS // 2, tm), lambda i, j, e: (i, 0, 0, 0, j),
                               pipeline_mode=once)],
        out_specs=tok(),
        out_shape=jax.ShapeDtypeStruct((b, s, d), BF16),
        scratch_shapes=[pltpu.VMEM((d, tm), F32), pltpu.VMEM((EB_PEER // EH_PEER, EH_PEER, tm), F32),
                        pltpu.VMEM((EB_PEER // EH_PEER, EH_PEER // 2, tm), jnp.uint32)],
        compiler_params=_params(("parallel", "parallel", "arbitrary")),
        name="peer_experts",
    )(h2, w["peer_u"], w["peer_vt"], rta, rtb)


def _final_kernel(x1_ref, po_ref, mod_ref, y_ref):
    y_ref[0] = x1_ref[0] + mod_ref[0, 5:6, :] * po_ref[0].astype(F32)


def _final(x1, po, mod):
    b, s, d = x1.shape
    tm = min(TM_FINAL, s)
    tok = lambda: pl.BlockSpec((1, tm, d), lambda i, j: (i, j, 0))
    return pl.pallas_call(
        _final_kernel,
        grid=(b, s // tm),
        in_specs=[tok(), tok(), pl.BlockSpec((1, 6, d), lambda i, j: (i, 0, 0))],
        out_specs=tok(),
        out_shape=jax.ShapeDtypeStruct((b, s, d), F32),
        compiler_params=_params(("parallel", "parallel")),
        name="peer_residual",
    )(x1, po, mod)


def _prepare_weights(norm1_w, norm2_w, w_in, q_a_norm, kv_a_norm, w_uq, w_uk, w_uv, q_norm, k_norm,
                     ret_decay_logit, ret_gn_w, w_o, peer_wq, peer_sub_keys, peer_u, peer_v):
    half = MLA_ROPE // 2

    def rope_lanes(t):
        z = jnp.zeros(t.shape[:-1] + (half,), t.dtype)
        return jnp.concatenate([t[..., :half], z, t[..., half:], z], axis=-1)

    def head_lanes(t):
        return jnp.concatenate([t[..., :MLA_NOPE], rope_lanes(t[..., MLA_NOPE:])], axis=-1)

    w_in = w_in[0]
    o = Q_LORA + KV_LORA
    w_in_p = jnp.concatenate([w_in[:, :o], rope_lanes(w_in[:, o:o + MLA_ROPE]), w_in[:, o + MLA_ROPE:]], axis=1)
    w_uq_p = head_lanes(w_uq[0].reshape(Q_LORA, MLA_HEADS, MLA_QK)).reshape(Q_LORA, MLA_HEADS * HEAD_PAD)
    row = lambda t: t.reshape(1, -1).astype(F32)
    return {
        "norm1": row(norm1_w[0]), "norm2": row(norm2_w[0]),
        "w_in": w_in_p.astype(BF16),
        "q_a_norm": row(q_a_norm[0]), "kv_a_norm": row(kv_a_norm[0]),
        "w_uq": w_uq_p.astype(BF16), "w_uk": w_uk[0].astype(BF16), "w_uv": w_uv[0].astype(BF16),
        "q_norm": row(head_lanes(q_norm[0])), "k_norm": row(head_lanes(k_norm[0])),
        "ret_logit": ret_decay_logit[0].T.reshape(RET_HEADS, 2, 1).astype(F32),
        "ret_gn": row(ret_gn_w[0]),
        "w_o": w_o[0].astype(BF16),
        "peer_wq": peer_wq[0].astype(BF16),
        "peer_keys": peer_sub_keys[0].astype(BF16),
        "peer_u": peer_u[0].astype(BF16),
        "peer_vt": peer_v[0].T.astype(BF16),
    }


def _trunk(x, mod, w):
    q, k, v, rq, rk, rv, rg = _premix(x, mod, w)
    a = _attention(q, k, v)
    r = _retention(rq, rk, rv, rg, w)
    x1, h2 = _postmix(x, a, r, mod, w)
    rta, rtb = _routing(h2, w)
    po = _peer(h2, rta, rtb, w)
    return _final(x1, po, mod)


def kernel(x_prompt, x_sample, c_prompt, c_sample, norm1_w, norm2_w, w_ada, b_ada, w_in, q_a_norm, kv_a_norm, w_uq, w_uk, w_uv, q_norm, k_norm, ret_decay_logit, ret_gn_w, w_o, peer_wq, peer_sub_keys, peer_u, peer_v):
    assert w_ada.shape[0] == 1, "single-layer trunk"
    d = x_prompt.shape[-1]
    w = _prepare_weights(norm1_w, norm2_w, w_in, q_a_norm, kv_a_norm, w_uq, w_uk, w_uv, q_norm, k_norm,
                         ret_decay_logit, ret_gn_w, w_o, peer_wq, peer_sub_keys, peer_u, peer_v)
    nbp = c_prompt.shape[0]
    mod = _adaln(jnp.concatenate([c_prompt, c_sample], axis=0), w_ada[0], b_ada[0]).reshape(-1, 6, d)
    y_prompt = _trunk(x_prompt, mod[:nbp], w)
    y_sample = _trunk(x_sample, mod[nbp:], w)
    return (y_prompt, y_sample)
```
